```python
import jax, jax.numpy as jnp
from jax import lax
import numpy as np

D_MODEL = 1024
BATCH = 2
SEQ = 16384
DEPTH = 1

A_HEADS = 8
A_KV_HEADS = 2
A_HEAD_DIM = 64
WINDOW = 128
BLOCK = 128
B_HEADS = 8
B_NOPE_DIM = 64
B_ROPE_DIM = 32
B_V_DIM = 64
B_Q_RANK = 768
B_KV_RANK = 256
ROPE_THETA = 10000.0
N_GROUPS = 4
EXPERTS_PER_GROUP = 8
N_EXPERTS = N_GROUPS * EXPERTS_PER_GROUP
EXPERT_FF = 256
TOP_K_IN_GROUP = 2
LN_EPS = 1e-5
RMS_EPS = 1e-6
DEEPNORM_ALPHA = (2 * DEPTH) ** 0.25
DEEPNORM_BETA = (8 * DEPTH) ** -0.25

A_Q_COLS = A_HEADS * A_HEAD_DIM
A_KV_COLS = A_KV_HEADS * A_HEAD_DIM
A_WIDTH = A_HEADS * A_HEAD_DIM
B_WIDTH = B_HEADS * B_V_DIM
MIX_WIDTH = A_WIDTH + B_WIDTH
IN_SPLITS = (A_Q_COLS,
             A_Q_COLS + A_KV_COLS,
             A_Q_COLS + 2 * A_KV_COLS,
             A_Q_COLS + 2 * A_KV_COLS + B_Q_RANK,
             A_Q_COLS + 2 * A_KV_COLS + B_Q_RANK + B_KV_RANK)
IN_COLS = A_Q_COLS + 2 * A_KV_COLS + B_Q_RANK + B_KV_RANK + B_ROPE_DIM
NEG_BIG = -1e30

kernel_name = "hybrid_swa_mla_hiermoe_deepnorm"


def layer_norm(x, g, b):
    xf = x.astype(jnp.float32)
    mu = jnp.mean(xf, -1, keepdims=True)
    xc = xf - mu
    var = jnp.mean(xc * xc, -1, keepdims=True)
    return (xc * lax.rsqrt(var + LN_EPS) * g.astype(jnp.float32) + b.astype(jnp.float32)).astype(x.dtype)


def rms_norm(x, g):
    xf = x.astype(jnp.float32)
    return (xf * lax.rsqrt(jnp.mean(xf * xf, -1, keepdims=True) + RMS_EPS) * g.astype(jnp.float32)).astype(x.dtype)


def rope_cos_sin(positions, dim):
    inv = 1.0 / (ROPE_THETA ** (jnp.arange(0, dim, 2, dtype=jnp.float32) / dim))
    ang = positions.astype(jnp.float32)[..., None] * inv
    return jnp.cos(ang), jnp.sin(ang)


def apply_rope(x, cos, sin):
    x1, x2 = jnp.split(x.astype(jnp.float32), 2, axis=-1)
    c = cos[:, :, None, :]
    s = sin[:, :, None, :]
    return jnp.concatenate([x1 * c - x2 * s, x1 * s + x2 * c], axis=-1).astype(x.dtype)


def windowed_gqa_sink(q, k, v, sink):
    b, s, h, d = q.shape
    kvh = k.shape[2]
    g = h // kvh
    nb = s // BLOCK
    qb = q.reshape(b, nb, BLOCK, kvh, g, d)

    def bands(t):
        tp = jnp.pad(t, ((0, 0), (BLOCK, BLOCK), (0, 0), (0, 0))).reshape(b, nb + 2, BLOCK, kvh, d)
        return jnp.concatenate([tp[:, :-2], tp[:, 1:-1], tp[:, 2:]], axis=2)

    kb = bands(k)
    vb = bands(v)
    scores = jnp.einsum('bnqkgd,bnskd->bnkgqs', qb, kb,
                        preferred_element_type=jnp.float32) * (d ** -0.5)
    qpos = jnp.arange(nb)[:, None] * BLOCK + jnp.arange(BLOCK)[None, :]
    kpos = jnp.arange(nb)[:, None] * BLOCK - BLOCK + jnp.arange(3 * BLOCK)[None, :]
    kp = kpos[:, None, :]
    valid = (kp >= 0) & (kp < s) & (jnp.abs(kp - qpos[:, :, None]) <= WINDOW)
    scores = jnp.where(valid[None, :, None, None], scores, NEG_BIG)
    sk = sink.astype(jnp.float32).reshape(1, 1, kvh, g, 1, 1)
    m = jnp.maximum(jnp.max(scores, -1, keepdims=True), sk)
    p = jnp.exp(scores - m)
    p = p / (jnp.sum(p, -1, keepdims=True) + jnp.exp(sk - m))
    out = jnp.einsum('bnkgqs,bnskd->bnqkgd', p.astype(v.dtype), vb)
    return out.reshape(b, s, h * d)


def mla_attention(q_nope, q_rope, k_nope, k_rope, v):
    b, s, h, dn = q_nope.shape
    dv = v.shape[-1]
    nb = s // BLOCK
    scale = (B_NOPE_DIM + B_ROPE_DIM) ** -0.5
    qn_blocks = q_nope.reshape(b, nb, BLOCK, h, dn).transpose(1, 0, 2, 3, 4)
    qr_blocks = q_rope.reshape(b, nb, BLOCK, h, q_rope.shape[-1]).transpose(1, 0, 2, 3, 4)

    def one_block(args):
        qn, qr = args
        sc = (jnp.einsum('bqhd,bshd->bhqs', qn, k_nope, preferred_element_type=jnp.float32)
              + jnp.einsum('bqhr,bsr->bhqs', qr, k_rope, preferred_element_type=jnp.float32)) * scale
        p = jax.nn.softmax(sc, axis=-1)
        return jnp.einsum('bhqs,bshd->bqhd', p.astype(v.dtype), v)

    out = lax.map(one_block, (qn_blocks, qr_blocks))
    return out.transpose(1, 0, 2, 3, 4).reshape(b, s, h * dv)


def hierarchical_moe(x, w_group, b_group, w_expert, b_expert, w_gate, w_up, w_down):
    b, s, d = x.shape
    t = x.reshape(b * s, d)
    gl = (t @ w_group).astype(jnp.float32) + b_group.astype(jnp.float32)
    pg = jax.nn.softmax(gl, axis=-1)
    _, g_idx = lax.top_k(gl, 1)
    p_group = jnp.take_along_axis(pg, g_idx, axis=-1)
    el = ((t @ w_expert).astype(jnp.float32) + b_expert.astype(jnp.float32)).reshape(-1, N_GROUPS, EXPERTS_PER_GROUP)
    el_sel = jnp.take_along_axis(el, g_idx[:, :, None], axis=1)[:, 0]
    top_v, top_i = lax.top_k(el_sel, TOP_K_IN_GROUP)
    w = p_group * jax.nn.softmax(top_v, axis=-1)
    e_idx = g_idx * EXPERTS_PER_GROUP + top_i
    combine = jnp.sum(jax.nn.one_hot(e_idx, N_EXPERTS, dtype=jnp.float32) * w[..., None], axis=1)
    y = jnp.zeros(t.shape, jnp.float32)
    for e in range(N_EXPERTS):
        hid = jax.nn.silu(t @ w_gate[e]) * (t @ w_up[e])
        y = y + combine[:, e:e + 1] * (hid @ w_down[e]).astype(jnp.float32)
    return y.astype(x.dtype).reshape(b, s, d)


def setup_inputs(seed: int = 0) -> dict:
    key = jax.random.key(seed)
    ks = jax.random.split(key, 26)
    L, D = DEPTH, D_MODEL
    f32 = jnp.float32

    def nrm(k, shape, scale):
        return jax.random.normal(k, shape, f32) * scale

    offs = jax.random.randint(ks[1], (BATCH, 1), 0, 4096, dtype=jnp.int32)
    positions = jnp.arange(SEQ, dtype=jnp.int32)[None, :] + offs
    return {
        "x": nrm(ks[0], (BATCH, SEQ, D), 1.0),
        "positions": positions,
        "ln_emb_g": 1.0 + nrm(ks[2], (D,), 0.02),
        "ln_emb_b": nrm(ks[3], (D,), 0.02),
        "w_in": nrm(ks[4], (L, D, IN_COLS), D ** -0.5),
        "a_sink": nrm(ks[5], (L, A_HEADS), 1.0),
        "q_a_norm_g": 1.0 + nrm(ks[6], (L, B_Q_RANK), 0.02),
        "w_q_b": nrm(ks[7], (L, B_Q_RANK, B_HEADS * (B_NOPE_DIM + B_ROPE_DIM)), B_Q_RANK ** -0.5),
        "kv_a_norm_g": 1.0 + nrm(ks[8], (L, B_KV_RANK), 0.02),
        "w_kv_b": nrm(ks[9], (L, B_KV_RANK, B_HEADS * (B_NOPE_DIM + B_V_DIM)), B_KV_RANK ** -0.5),
        "out_norm_a_g": 1.0 + nrm(ks[10], (L, A_WIDTH), 0.02),
        "out_norm_b_g": 1.0 + nrm(ks[11], (L, B_WIDTH), 0.02),
        "w_out": nrm(ks[12], (L, MIX_WIDTH, D), MIX_WIDTH ** -0.5 * DEEPNORM_BETA),
        "ln_attn_g": 1.0 + nrm(ks[13], (L, D), 0.02),
        "ln_attn_b": nrm(ks[14], (L, D), 0.02),
        "w_group": nrm(ks[15], (L, D, N_GROUPS), D ** -0.5),
        "b_group": nrm(ks[16], (L, N_GROUPS), 0.01),
        "w_expert": nrm(ks[17], (L, D, N_EXPERTS), D ** -0.5),
        "b_expert": nrm(ks[18], (L, N_EXPERTS), 0.01),
        "w_gate": nrm(ks[19], (L, N_EXPERTS, D, EXPERT_FF), D ** -0.5),
        "w_up": nrm(ks[20], (L, N_EXPERTS, D, EXPERT_FF), D ** -0.5),
        "w_down": nrm(ks[21], (L, N_EXPERTS, EXPERT_FF, D), EXPERT_FF ** -0.5 * DEEPNORM_BETA),
        "ln_ffn_g": 1.0 + nrm(ks[22], (L, D), 0.02),
        "ln_ffn_b": nrm(ks[23], (L, D), 0.02),
    }


def reference(x, positions, ln_emb_g, ln_emb_b, w_in, a_sink, q_a_norm_g, w_q_b, kv_a_norm_g, w_kv_b,
              out_norm_a_g, out_norm_b_g, w_out, ln_attn_g, ln_attn_b, w_group, b_group, w_expert,
              b_expert, w_gate, w_up, w_down, ln_ffn_g, ln_ffn_b):
    b, s, _ = x.shape
    h = layer_norm(x, ln_emb_g, ln_emb_b)
    cos_a, sin_a = rope_cos_sin(positions, A_HEAD_DIM)
    cos_b, sin_b = rope_cos_sin(positions, B_ROPE_DIM)
    for l in range(DEPTH):
        proj = h @ w_in[l]
        qa, ka, va, cq, ckv, kr = jnp.split(proj, IN_SPLITS, axis=-1)
        qa = apply_rope(qa.reshape(b, s, A_HEADS, A_HEAD_DIM), cos_a, sin_a)
        ka = apply_rope(ka.reshape(b, s, A_KV_HEADS, A_HEAD_DIM), cos_a, sin_a)
        va = va.reshape(b, s, A_KV_HEADS, A_HEAD_DIM)
        out_a = windowed_gqa_sink(qa, ka, va, a_sink[l])
        qb = (rms_norm(cq, q_a_norm_g[l]) @ w_q_b[l]).reshape(b, s, B_HEADS, B_NOPE_DIM + B_ROPE_DIM)
        q_nope = qb[..., :B_NOPE_DIM]
        q_rope = apply_rope(qb[..., B_NOPE_DIM:], cos_b, sin_b)
        kv = (rms_norm(ckv, kv_a_norm_g[l]) @ w_kv_b[l]).reshape(b, s, B_HEADS, B_NOPE_DIM + B_V_DIM)
        k_nope = kv[..., :B_NOPE_DIM]
        v_b = kv[..., B_NOPE_DIM:]
        k_rope = apply_rope(kr[:, :, None, :], cos_b, sin_b)[:, :, 0]
        out_b = mla_attention(q_nope, q_rope, k_nope, k_rope, v_b)
        mixed = jnp.concatenate([rms_norm(out_a, out_norm_a_g[l]), rms_norm(out_b, out_norm_b_g[l])], axis=-1) @ w_out[l]
        h = layer_norm(DEEPNORM_ALPHA * h + mixed, ln_attn_g[l], ln_attn_b[l])
        ffn = hierarchical_moe(h, w_group[l], b_group[l], w_expert[l], b_expert[l], w_gate[l], w_up[l], w_down[l])
        h = layer_norm(DEEPNORM_ALPHA * h + ffn, ln_ffn_g[l], ln_ffn_b[l])
    return h
```

```python
import functools

import numpy as np
import jax
import jax.numpy as jnp
from jax import lax
from jax.experimental import pallas as pl
from jax.experimental.pallas import tpu as pltpu

F32 = jnp.float32
BF16 = jnp.bfloat16

D_MODEL = 1024
A_HEADS = 8
A_KV_HEADS = 2
A_HEAD_DIM = 64
WINDOW = 128
B_HEADS = 8
B_NOPE = 64
B_ROPE = 32
B_V = 64
B_Q_RANK = 768
B_KV_RANK = 256
ROPE_THETA = 10000.0
N_GROUPS = 4
EXPERTS_PER_GROUP = 8
N_EXPERTS = N_GROUPS * EXPERTS_PER_GROUP
EXPERT_FF = 256
LN_EPS = 1e-5
RMS_EPS = 1e-6
DEPTH = 1
ALPHA = (2 * DEPTH) ** 0.25
NEG_BIG = -1e30

LANES = 128
A_Q_COLS = A_HEADS * A_HEAD_DIM
A_KV_COLS = A_KV_HEADS * A_HEAD_DIM
PAD_COLS = B_HEADS * LANES

C_QA = 0
C_KA = C_QA + A_Q_COLS
C_VA = C_KA + 2 * A_KV_HEADS * LANES
C_CQ = C_VA + 2 * A_KV_HEADS * LANES
C_CKV = C_CQ + B_Q_RANK
C_KR = C_CKV + B_KV_RANK
W1_COLS = C_KR + LANES

TM_PROJ = 256
TQ_A = 256
TQ_B = 512
TK_B = 512
TE = 256
TM_SLOT = 1024
TM_DISP = 512
TM_COMB = 256
VMEM_LIMIT = 56 * 1024 * 1024


def _cparams(sem):
    return pltpu.CompilerParams(dimension_semantics=sem, vmem_limit_bytes=VMEM_LIMIT)


def _layer_norm(x, g, b):
    mu = jnp.mean(x, -1, keepdims=True)
    xc = x - mu
    var = jnp.mean(xc * xc, -1, keepdims=True)
    return xc * lax.rsqrt(var + LN_EPS) * g + b


def _rope_block(xc, cos, sin_signed, first_half, half):
    rot = jnp.where(first_half, pltpu.roll(xc, LANES - half, 1), pltpu.roll(xc, half, 1))
    return xc * cos + rot * sin_signed


def _rope_table_body(pos_ref, inva_ref, sga_ref, invb_ref, sgb_ref, cosa_ref, sina_ref, cosb_ref, sinb_ref):
    pos = pos_ref[...].astype(F32)
    ang_a = pos * inva_ref[...]
    cosa_ref[...] = jnp.cos(ang_a)
    sina_ref[...] = jnp.sin(ang_a) * sga_ref[...]
    ang_b = pos * invb_ref[...]
    cosb_ref[...] = jnp.cos(ang_b)
    sinb_ref[...] = jnp.sin(ang_b) * sgb_ref[...]


def _rope_tables(positions):
    t = positions.shape[0] * positions.shape[1]
    tm = 512
    lane = np.arange(LANES)
    inv_a = 1.0 / (ROPE_THETA ** (np.arange(0, A_HEAD_DIM, 2, dtype=np.float64) / A_HEAD_DIM))
    inv_b = 1.0 / (ROPE_THETA ** (np.arange(0, B_ROPE, 2, dtype=np.float64) / B_ROPE))
    inva = inv_a[lane % (A_HEAD_DIM // 2)]
    sga = np.where(lane % A_HEAD_DIM < A_HEAD_DIM // 2, -1.0, 1.0)
    in_rope = (lane >= B_NOPE) & (lane < B_NOPE + B_ROPE)
    invb = np.where(in_rope, inv_b[(lane - B_NOPE) % (B_ROPE // 2)], 0.0)
    sgb = np.where(lane - B_NOPE < B_ROPE // 2, -1.0, 1.0)
    consts = [jnp.asarray(a.reshape(1, LANES), F32) for a in (inva, sga, invb, sgb)]
    row = pl.BlockSpec((1, LANES), lambda i: (0, 0))
    tab = pl.BlockSpec((tm, LANES), lambda i: (i, 0))
    return pl.pallas_call(
        _rope_table_body,
        grid=(t // tm,),
        in_specs=[pl.BlockSpec((tm, 1), lambda i: (i, 0)), row, row, row, row],
        out_specs=[tab, tab, tab, tab],
        out_shape=[jax.ShapeDtypeStruct((t, LANES), F32)] * 4,
        compiler_params=_cparams(("parallel",)),
        name="rope_tables",
    )(positions.reshape(t, 1), *consts)


def _proj_body(x_ref, g_ref, b_ref, w1_ref, wq_ref, wk_ref, wv_ref, gq_ref, gkv_ref,
               cosa_ref, sina_ref, cosb_ref, sinb_ref,
               qa_ref, ka_ref, va_ref, qb_ref, kb_ref, vb_ref):
    h = _layer_norm(x_ref[...], g_ref[...], b_ref[...]).astype(BF16)
    lane = lax.broadcasted_iota(jnp.int32, (1, LANES), 1)
    first_a = (lane % A_HEAD_DIM) < (A_HEAD_DIM // 2)
    first_b = (lane % B_ROPE) < (B_ROPE // 2)
    cosa, sina = cosa_ref[...], sina_ref[...]
    cosb, sinb = cosb_ref[...], sinb_ref[...]

    def proj(c0, width):
        return jnp.dot(h, w1_ref[:, c0:c0 + width], preferred_element_type=F32)

    qa = proj(C_QA, A_Q_COLS)
    for c in range(A_Q_COLS // LANES):
        blk = _rope_block(qa[:, c * LANES:(c + 1) * LANES], cosa, sina, first_a, A_HEAD_DIM // 2)
        qa_ref[:, c * LANES:(c + 1) * LANES] = (blk * (A_HEAD_DIM ** -0.5)).astype(BF16)
    ka = proj(C_KA, 2 * A_KV_HEADS * LANES)
    for c in range(2 * A_KV_HEADS):
        blk = _rope_block(ka[:, c * LANES:(c + 1) * LANES], cosa, sina, first_a, A_HEAD_DIM // 2)
        ka_ref[:, c * LANES:(c + 1) * LANES] = blk.astype(BF16)
    va_ref[...] = proj(C_VA, 2 * A_KV_HEADS * LANES).astype(BF16)

    cq = proj(C_CQ, B_Q_RANK)
    cq = cq * lax.rsqrt(jnp.mean(cq * cq, -1, keepdims=True) + RMS_EPS) * gq_ref[...]
    qb = jnp.dot(cq.astype(BF16), wq_ref[...], preferred_element_type=F32)
    scale_b = (B_NOPE + B_ROPE) ** -0.5
    for c in range(B_HEADS):
        blk = _rope_block(qb[:, c * LANES:(c + 1) * LANES], cosb, sinb, first_b, B_ROPE // 2)
        qb_ref[:, c * LANES:(c + 1) * LANES] = (blk * scale_b).astype(BF16)

    ckv = proj(C_CKV, B_KV_RANK)
    ckv = (ckv * lax.rsqrt(jnp.mean(ckv * ckv, -1, keepdims=True) + RMS_EPS) * gkv_ref[...]).astype(BF16)
    kr = _rope_block(proj(C_KR, LANES), cosb, sinb, first_b, B_ROPE // 2)
    kn = jnp.dot(ckv, wk_ref[...], preferred_element_type=F32)
    for c in range(B_HEADS):
        kb_ref[:, c * LANES:(c + 1) * LANES] = (kn[:, c * LANES:(c + 1) * LANES] + kr).astype(BF16)
    vb_ref[...] = jnp.dot(ckv, wv_ref[...], preferred_element_type=F32).astype(BF16)


def _lo_hi(w):
    d = w.shape[0]
    w = w.reshape(d, A_KV_HEADS, A_HEAD_DIM)
    z = jnp.zeros_like(w)
    return jnp.concatenate([w, z, z, w], -1).reshape(d, A_KV_HEADS * 2 * LANES)


def _proj_weights(w_in, w_q_b, w_kv_b):
    w = w_in
    d = w.shape[0]
    o = A_Q_COLS
    ka = w[:, o:o + A_KV_COLS]; o += A_KV_COLS
    va = w[:, o:o + A_KV_COLS]; o += A_KV_COLS
    cq = w[:, o:o + B_Q_RANK]; o += B_Q_RANK
    ckv = w[:, o:o + B_KV_RANK]; o += B_KV_RANK
    kr = w[:, o:o + B_ROPE]
    kr_blk = jnp.concatenate([jnp.zeros((d, B_NOPE), w.dtype), kr,
                              jnp.zeros((d, LANES - B_NOPE - B_ROPE), w.dtype)], -1)
    w1 = jnp.concatenate([w[:, :A_Q_COLS], _lo_hi(ka), _lo_hi(va), cq, ckv, kr_blk], -1).astype(BF16)
    wq = w_q_b.reshape(B_Q_RANK, B_HEADS, B_NOPE + B_ROPE)
    wq = jnp.pad(wq, ((0, 0), (0, 0), (0, LANES - B_NOPE - B_ROPE))).reshape(B_Q_RANK, PAD_COLS).astype(BF16)
    wkv = w_kv_b.reshape(B_KV_RANK, B_HEADS, B_NOPE + B_V)
    wk = jnp.pad(wkv[:, :, :B_NOPE], ((0, 0), (0, 0), (0, LANES - B_NOPE))).reshape(B_KV_RANK, PAD_COLS).astype(BF16)
    wv = jnp.pad(wkv[:, :, B_NOPE:], ((0, 0), (0, 0), (0, LANES - B_V))).reshape(B_KV_RANK, PAD_COLS).astype(BF16)
    return w1, wq, wk, wv


def _projections(x2, ln_g, ln_b, w1, wq, wk, wv, gq, gkv, tables):
    t = x2.shape[0]
    tm = TM_PROJ
    full = lambda a: pl.BlockSpec(a.shape, lambda i: (0,) * a.ndim)
    tile = lambda n: pl.BlockSpec((tm, n), lambda i: (i, 0))
    consts = [ln_g, ln_b, w1, wq, wk, wv, gq, gkv]
    out_cols = [A_Q_COLS, 2 * A_KV_HEADS * LANES, 2 * A_KV_HEADS * LANES, PAD_COLS, PAD_COLS, PAD_COLS]
    return pl.pallas_call(
        _proj_body,
        grid=(t // tm,),
        in_specs=[tile(D_MODEL)] + [full(a) for a in consts] + [tile(LANES)] * 4,
        out_specs=[tile(n) for n in out_cols],
        out_shape=[jax.ShapeDtypeStruct((t, n), BF16) for n in out_cols],
        compiler_params=_cparams(("parallel",)),
        name="ln_in_proj",
    )(x2, *consts, *tables)


def _attn_a_body(seq, sink_ref, q_ref, k0, k1, k2, k3, v0, v1, v2, v3, o_ref):
    i = pl.program_id(1)
    q = q_ref[...]
    kk = jnp.concatenate([k0[...], k1[...], k2[...], k3[...]], 0)
    vv = jnp.concatenate([v0[...], v1[...], v2[...], v3[...]], 0)
    nk = kk.shape[0]
    qpos = i * TQ_A + lax.broadcasted_iota(jnp.int32, (TQ_A, 1), 0)
    kpos = i * TQ_A - WINDOW + lax.broadcasted_iota(jnp.int32, (1, nk), 1)
    valid = (kpos >= 0) & (kpos < seq) & (jnp.abs(kpos - qpos) <= WINDOW)
    for pair in range(A_HEADS // 2):
        g = (2 * pair) // (A_HEADS // A_KV_HEADS)
        qp = q[:, pair * LANES:(pair + 1) * LANES]
        o = jnp.zeros((TQ_A, LANES), F32)
        for par in range(2):
            c = (2 * g + par) * LANES
            sink = sink_ref[0, 2 * pair + par]
            s = lax.dot_general(qp, kk[:, c:c + LANES], (((1,), (1,)), ((), ())), preferred_element_type=F32)
            s = jnp.where(valid, s, NEG_BIG)
            m = jnp.maximum(jnp.max(s, -1, keepdims=True), sink)
            e = jnp.exp(s - m)
            den = jnp.sum(e, -1, keepdims=True) + jnp.exp(sink - m)
            p = (e / den).astype(BF16)
            o = o + jnp.dot(p, vv[:, c:c + LANES], preferred_element_type=F32)
        o_ref[:, pair * LANES:(pair + 1) * LANES] = o


def _attention_a(qa, ka, va, sink, batch, seq):
    t = qa.shape[0]
    nq = seq // TQ_A
    nb = seq // WINDOW
    per = TQ_A // WINDOW

    def kv_spec(j):
        def imap(b, i):
            return (b * nb + jnp.clip(per * i - 1 + j, 0, nb - 1), 0)
        return pl.BlockSpec((WINDOW, 2 * A_KV_HEADS * LANES), imap)

    kv_specs = [kv_spec(j) for j in range(per + 2)]
    return pl.pallas_call(
        functools.partial(_attn_a_body, seq),
        grid=(batch, nq),
        in_specs=[pl.BlockSpec(memory_space=pltpu.SMEM),
                  pl.BlockSpec((TQ_A, A_Q_COLS), lambda b, i: (b * nq + i, 0))] + kv_specs + kv_specs,
        out_specs=pl.BlockSpec((TQ_A, A_Q_COLS), lambda b, i: (b * nq + i, 0)),
        out_shape=jax.ShapeDtypeStruct((t, A_Q_COLS), F32),
        compiler_params=_cparams(("parallel", "parallel")),
        name="window_attn",
    )(sink, qa, *([ka] * (per + 2)), *([va] * (per + 2)))


def _attn_b_body(nkt, q_ref, k_ref, v_ref, o_ref, m_sc, l_sc, acc_sc):
    q = q_ref[...]
    m_sc[...] = jnp.full(m_sc.shape, -jnp.inf, F32)
    l_sc[...] = jnp.zeros(l_sc.shape, F32)
    acc_sc[...] = jnp.zeros(acc_sc.shape, F32)

    def step(j, carry):
        start = pl.multiple_of(j * TK_B, TK_B)
        k = k_ref[pl.ds(start, TK_B), :]
        v = v_ref[pl.ds(start, TK_B), :]
        s = lax.dot_general(q, k, (((1,), (1,)), ((), ())), preferred_element_type=F32)
        m_prev = m_sc[...]
        m_new = jnp.maximum(m_prev, jnp.max(s, -1, keepdims=True))
        alpha = jnp.exp(m_prev - m_new)
        p = jnp.exp(s - m_new)
        l_sc[...] = alpha * l_sc[...] + jnp.sum(p, -1, keepdims=True)
        acc_sc[...] = alpha * acc_sc[...] + jnp.dot(p.astype(BF16), v, preferred_element_type=F32)
        m_sc[...] = m_new
        return carry

    lax.fori_loop(0, nkt, step, 0)
    o_ref[...] = acc_sc[...] / l_sc[...]


def _attention_b(qb, kb, vb, batch, seq):
    t = qb.shape[0]
    nq = seq // TQ_B
    return pl.pallas_call(
        functools.partial(_attn_b_body, seq // TK_B),
        grid=(batch * B_HEADS, nq),
        in_specs=[pl.BlockSpec((TQ_B, LANES), lambda bh, i: ((bh // B_HEADS) * nq + i, bh % B_HEADS)),
                  pl.BlockSpec((seq, LANES), lambda bh, i: (bh // B_HEADS, bh % B_HEADS)),
                  pl.BlockSpec((seq, LANES), lambda bh, i: (bh // B_HEADS, bh % B_HEADS))],
        out_specs=pl.BlockSpec((TQ_B, LANES), lambda bh, i: ((bh // B_HEADS) * nq + i, bh % B_HEADS)),
        out_shape=jax.ShapeDtypeStruct((t, PAD_COLS), F32),
        scratch_shapes=[pltpu.VMEM((TQ_B, 1), F32), pltpu.VMEM((TQ_B, 1), F32), pltpu.VMEM((TQ_B, LANES), F32)],
        compiler_params=_cparams(("parallel", "arbitrary")),
        name="mla_attn",
    )(qb, kb, vb)


R_E1, R_E2, R_RANK1, R_RANK2, R_W1, R_W2 = range(6)
GROUP_LANE0 = N_EXPERTS


def _lane_pick(slab, lane, idx):
    return jnp.sum(jnp.where(lane == idx, slab, 0.0), -1, keepdims=True)


def _outproj_body(oa_ref, ob_ref, x_ref, ga_ref, gb_ref, woa_ref, wob_ref, lg0_ref, lb0_ref, lg1_ref, lb1_ref,
                  wrh_ref, wrl_ref, br_ref, h1_ref, h1p_ref, route_ref, counts_ref, carry_sc):
    @pl.when(pl.program_id(0) == 0)
    def _():
        carry_sc[...] = jnp.zeros(carry_sc.shape, F32)

    oa = oa_ref[...]
    ob = ob_ref[...]
    na = oa * lax.rsqrt(jnp.mean(oa * oa, -1, keepdims=True) + RMS_EPS) * ga_ref[...]
    nb = ob * lax.rsqrt(jnp.sum(ob * ob, -1, keepdims=True) * (1.0 / (B_HEADS * B_V)) + RMS_EPS) * gb_ref[...]
    mixed = (jnp.dot(na.astype(BF16), woa_ref[...], preferred_element_type=F32)
             + jnp.dot(nb.astype(BF16), wob_ref[...], preferred_element_type=F32))
    h0 = _layer_norm(x_ref[...], lg0_ref[...], lb0_ref[...])
    h1 = _layer_norm(ALPHA * h0 + mixed, lg1_ref[...], lb1_ref[...])
    h1_ref[...] = h1

    hi = h1.astype(BF16)
    hi32 = lax.bitcast_convert_type(hi.astype(F32), jnp.uint32)
    half = D_MODEL // 2
    h1p_ref[...] = hi32[:, :half] | (hi32[:, half:] >> 16)

    lo = (h1 - hi.astype(F32)).astype(BF16)
    logits = (jnp.dot(hi, wrh_ref[...], preferred_element_type=F32)
              + jnp.dot(lo, wrh_ref[...], preferred_element_type=F32)
              + jnp.dot(hi, wrl_ref[...], preferred_element_type=F32)) + br_ref[...]

    tm = logits.shape[0]
    lane = lax.broadcasted_iota(jnp.int32, (1, LANES), 1).astype(F32)
    far = float(LANES)
    is_g = (lane >= GROUP_LANE0) & (lane < GROUP_LANE0 + N_GROUPS)
    gl = jnp.where(is_g, logits, -jnp.inf)
    gmax = jnp.max(gl, -1, keepdims=True)
    gidx = jnp.min(jnp.where(gl == gmax, lane, far), -1, keepdims=True) - GROUP_LANE0
    pg = 1.0 / jnp.sum(jnp.where(is_g, jnp.exp(jnp.where(is_g, logits, gmax) - gmax), 0.0), -1, keepdims=True)
    in_grp = (lane >= gidx * EXPERTS_PER_GROUP) & (lane < (gidx + 1.0) * EXPERTS_PER_GROUP)
    el = jnp.where(in_grp, logits, -jnp.inf)
    v1 = jnp.max(el, -1, keepdims=True)
    i1 = jnp.min(jnp.where(el == v1, lane, far), -1, keepdims=True)
    el2 = jnp.where(lane == i1, -jnp.inf, el)
    v2 = jnp.max(el2, -1, keepdims=True)
    i2 = jnp.min(jnp.where(el2 == v2, lane, far), -1, keepdims=True)
    ex = jnp.exp(v2 - v1)
    w1 = pg / (1.0 + ex)
    w2 = pg * ex / (1.0 + ex)

    onehot = jnp.where((lane == i1) | (lane == i2), 1.0, 0.0)
    r_i = lax.broadcasted_iota(jnp.int32, (tm, tm), 0)
    c_i = lax.broadcasted_iota(jnp.int32, (tm, tm), 1)
    lower = jnp.where(c_i < r_i, 1.0, 0.0).astype(BF16)
    prefix = jnp.dot(lower, onehot.astype(BF16), preferred_element_type=F32) + carry_sc[0:1, :]
    rank1 = _lane_pick(prefix, lane, i1)
    rank2 = _lane_pick(prefix, lane, i2)
    new_carry = carry_sc[0:1, :] + jnp.sum(onehot, 0, keepdims=True)
    carry_sc[...] = jnp.broadcast_to(new_carry, carry_sc.shape)
    counts_ref[...] = jnp.broadcast_to(new_carry, counts_ref.shape)

    slab = jnp.zeros((tm, LANES), F32)
    for idx, val in ((R_E1, i1), (R_E2, i2), (R_RANK1, rank1), (R_RANK2, rank2), (R_W1, w1), (R_W2, w2)):
        slab = jnp.where(lane == idx, val, slab)
    route_ref[...] = slab


def _outproj_router(oa, ob, x2, consts):
    t = x2.shape[0]
    tm = TM_PROJ
    full = lambda a: pl.BlockSpec(a.shape, lambda i: (0,) * a.ndim)
    tile = lambda n: pl.BlockSpec((tm, n), lambda i: (i, 0))
    return pl.pallas_call(
        _outproj_body,
        grid=(t // tm,),
        in_specs=[tile(A_Q_COLS), tile(PAD_COLS), tile(D_MODEL)] + [full(a) for a in consts],
        out_specs=[tile(D_MODEL), tile(D_MODEL // 2), tile(LANES), pl.BlockSpec((8, LANES), lambda i: (0, 0))],
        out_shape=[jax.ShapeDtypeStruct((t, D_MODEL), F32), jax.ShapeDtypeStruct((t, D_MODEL // 2), jnp.uint32),
                   jax.ShapeDtypeStruct((t, LANES), F32), jax.ShapeDtypeStruct((8, LANES), F32)],
        scratch_shapes=[pltpu.VMEM((8, LANES), F32)],
        compiler_params=_cparams(("arbitrary",)),
        name="out_proj_router",
    )(oa, ob, x2, *consts)


def _slot_body(n_tiles, route_ref, counts_ref, slots_ref, tmap_ref):
    lane_i = lax.broadcasted_iota(jnp.int32, (1, LANES), 1)
    lane = lane_i.astype(F32)
    cnt = jnp.where(lane_i < N_EXPERTS, counts_ref[0:1, :], 0.0)
    padded = jnp.floor((cnt + (TE - 1)) * (1.0 / TE)) * TE
    incl = padded
    for k in (1, 2, 4, 8, 16):
        incl = incl + jnp.where(lane_i >= k, pltpu.roll(incl, k, 1), 0.0)
    off = incl - padded

    route = route_ref[...]
    e1 = _lane_pick(route, lane, R_E1)
    e2 = _lane_pick(route, lane, R_E2)
    s1 = _lane_pick(jnp.broadcast_to(off, route.shape), lane, e1) + _lane_pick(route, lane, R_RANK1)
    s2 = _lane_pick(jnp.broadcast_to(off, route.shape), lane, e2) + _lane_pick(route, lane, R_RANK2)
    slab = jnp.where(lane == 0, s1, jnp.where(lane == 1, s2, 0.0))
    slots_ref[...] = slab.astype(jnp.int32)

    row0 = lax.broadcasted_iota(jnp.int32, (n_tiles, 1), 0).astype(F32) * TE
    is_e = lane_i < N_EXPERTS
    te = jnp.sum(jnp.where(is_e & (incl <= row0), 1.0, 0.0), -1, keepdims=True)
    te = jnp.minimum(te, N_EXPERTS - 1.0)
    used = jnp.sum(jnp.where(lane == te, cnt - (row0 - off), 0.0), -1, keepdims=True)
    used = jnp.clip(used, 0.0, float(TE))
    tmap = jnp.where(lane == 0, te, jnp.where(lane == 1, used, 0.0))
    tmap_ref[...] = tmap.astype(jnp.int32)


def _slot_assign(route, counts, n_tiles):
    t = route.shape[0]
    tm = min(TM_SLOT, t)
    return pl.pallas_call(
        functools.partial(_slot_body, n_tiles),
        grid=(t // tm,),
        in_specs=[pl.BlockSpec((tm, LANES), lambda i: (i, 0)), pl.BlockSpec((8, LANES), lambda i: (0, 0))],
        out_specs=[pl.BlockSpec((tm, LANES), lambda i: (i, 0)), pl.BlockSpec((n_tiles, LANES), lambda i: (0, 0))],
        out_shape=[jax.ShapeDtypeStruct((t, LANES), jnp.int32), jax.ShapeDtypeStruct((n_tiles, LANES), jnp.int32)],
        compiler_params=_cparams(("arbitrary",)),
        name="slot_assign",
    )(route, counts)


def _dispatch_body(tm, slots_ref, src_ref, zeros_ref, dst_ref, sem):
    del zeros_ref
    base = pl.program_id(0) * tm

    def row_copy(j, k):
        return pltpu.make_async_copy(src_ref.at[pl.ds(base + j, 1), :],
                                     dst_ref.at[pl.ds(slots_ref[0, k, j], 1), :], sem)

    def start(j, carry):
        row_copy(j, 0).start()
        row_copy(j, 1).start()
        return carry

    def wait(j, carry):
        row_copy(j, 0).wait()
        row_copy(j, 1).wait()
        return carry

    lax.fori_loop(0, tm, start, 0, unroll=8)
    lax.fori_loop(0, tm, wait, 0, unroll=8)


def _dispatch(slots3, h1p, n_rows):
    t = h1p.shape[0]
    tm = slots3.shape[2]
    return pl.pallas_call(
        functools.partial(_dispatch_body, tm),
        grid=(t // tm,),
        in_specs=[pl.BlockSpec((1, 2, tm), lambda i: (i, 0, 0), memory_space=pltpu.SMEM),
                  pl.BlockSpec(memory_space=pl.ANY), pl.BlockSpec(memory_space=pl.ANY)],
        out_specs=pl.BlockSpec(memory_space=pl.ANY),
        out_shape=jax.ShapeDtypeStruct((n_rows, h1p.shape[1]), h1p.dtype),
        scratch_shapes=[pltpu.SemaphoreType.DMA(())],
        input_output_aliases={2: 0},
        compiler_params=_cparams(("arbitrary",)),
        name="moe_dispatch",
    )(slots3, h1p, jnp.zeros((n_rows, h1p.shape[1]), h1p.dtype))


def _expert_body(te_ref, used_ref, xs_ref, wg_ref, wu_ref, wd_ref, y_ref, wg_sc, wu_sc, wd_sc):
    r = pl.program_id(0)
    used = used_ref[r]
    prev = te_ref[jnp.maximum(r - 1, 0)]

    @pl.when((r == 0) | (te_ref[r] != prev))
    def _():
        wg_sc[...] = wg_ref[0].astype(BF16)
        wu_sc[...] = wu_ref[0].astype(BF16)
        wd_sc[...] = wd_ref[0].astype(BF16)

    @pl.when(used > 0)
    def _():
        words = xs_ref[...]
        xa = lax.bitcast_convert_type(words & jnp.uint32(0xFFFF0000), F32).astype(BF16)
        xb = lax.bitcast_convert_type(words << 16, F32).astype(BF16)
        half = D_MODEL // 2
        g = (jnp.dot(xa, wg_sc[:half, :], preferred_element_type=F32)
             + jnp.dot(xb, wg_sc[half:, :], preferred_element_type=F32))
        u = (jnp.dot(xa, wu_sc[:half, :], preferred_element_type=F32)
             + jnp.dot(xb, wu_sc[half:, :], preferred_element_type=F32))
        hid = (g * (1.0 / (1.0 + jnp.exp(-g))) * u).astype(BF16)
        y_ref[...] = jnp.dot(hid, wd_sc[...], preferred_element_type=F32)

    @pl.when(used <= 0)
    def _():
        y_ref[...] = jnp.zeros(y_ref.shape, F32)


def _expert_mlp(tile_expert, tile_used, xs, w_gate, w_up, w_down):
    n_rows = xs.shape[0]
    n_tiles = n_rows // TE
    grid_spec = pltpu.PrefetchScalarGridSpec(
        num_scalar_prefetch=2,
        grid=(n_tiles,),
        in_specs=[pl.BlockSpec((TE, D_MODEL // 2), lambda r, te, us: (r, 0)),
                  pl.BlockSpec((1, D_MODEL, EXPERT_FF), lambda r, te, us: (te[r], 0, 0)),
                  pl.BlockSpec((1, D_MODEL, EXPERT_FF), lambda r, te, us: (te[r], 0, 0)),
                  pl.BlockSpec((1, EXPERT_FF, D_MODEL), lambda r, te, us: (te[r], 0, 0))],
        out_specs=pl.BlockSpec((TE, D_MODEL), lambda r, te, us: (r, 0)),
        scratch_shapes=[pltpu.VMEM((D_MODEL, EXPERT_FF), BF16), pltpu.VMEM((D_MODEL, EXPERT_FF), BF16),
                        pltpu.VMEM((EXPERT_FF, D_MODEL), BF16)],
    )
    return pl.pallas_call(
        _expert_body,
        grid_spec=grid_spec,
        out_shape=jax.ShapeDtypeStruct((n_rows, D_MODEL), F32),
        compiler_params=_cparams(("arbitrary",)),
        name="expert_mlp",
    )(tile_expert, tile_used, xs, w_gate, w_up, w_down)


def _combine_body(tm, slots_ref, h1_ref, route_ref, g_ref, b_ref, y_ref, o_ref, ybuf, sem):
    def row_copy(j, k):
        return pltpu.make_async_copy(y_ref.at[pl.ds(slots_ref[0, k, j], 1), :],
                                     ybuf.at[k, pl.ds(j, 1), :], sem)

    def start(j, carry):
        row_copy(j, 0).start()
        row_copy(j, 1).start()
        return carry

    def wait(j, carry):
        row_copy(j, 0).wait()
        row_copy(j, 1).wait()
        return carry

    lax.fori_loop(0, tm, start, 0, unroll=8)
    lax.fori_loop(0, tm, wait, 0, unroll=8)

    lane = lax.broadcasted_iota(jnp.int32, (1, LANES), 1).astype(F32)
    route = route_ref[...]
    w1 = _lane_pick(route, lane, R_W1)
    w2 = _lane_pick(route, lane, R_W2)
    ffn = w1 * ybuf[0] + w2 * ybuf[1]
    o_ref[...] = _layer_norm(ALPHA * h1_ref[...] + ffn, g_ref[...], b_ref[...])


def _combine(slots3, h1, route, ln_g, ln_b, y):
    t = h1.shape[0]
    tm = slots3.shape[2]
    return pl.pallas_call(
        functools.partial(_combine_body, tm),
        grid=(t // tm,),
        in_specs=[pl.BlockSpec((1, 2, tm), lambda i: (i, 0, 0), memory_space=pltpu.SMEM),
                  pl.BlockSpec((tm, D_MODEL), lambda i: (i, 0)),
                  pl.BlockSpec((tm, LANES), lambda i: (i, 0)),
                  pl.BlockSpec((1, D_MODEL), lambda i: (0, 0)),
                  pl.BlockSpec((1, D_MODEL), lambda i: (0, 0)),
                  pl.BlockSpec(memory_space=pl.ANY)],
        out_specs=pl.BlockSpec((tm, D_MODEL), lambda i: (i, 0)),
        out_shape=jax.ShapeDtypeStruct((t, D_MODEL), F32),
        scratch_shapes=[pltpu.VMEM((2, tm, D_MODEL), F32), pltpu.SemaphoreType.DMA(())],
        compiler_params=_cparams(("arbitrary",)),
        name="moe_combine",
    )(slots3, h1, route, ln_g, ln_b, y)


def _slot_blocks(slots, tm):
    t = slots.shape[0]
    return slots[:, :2].reshape(t // tm, tm, 2).transpose(0, 2, 1)


def kernel(x, positions, ln_emb_g, ln_emb_b, w_in, a_sink, q_a_norm_g, w_q_b, kv_a_norm_g, w_kv_b, out_norm_a_g, out_norm_b_g, w_out, ln_attn_g, ln_attn_b, w_group, b_group, w_expert, b_expert, w_gate, w_up, w_down, ln_ffn_g, ln_ffn_b):
    batch, seq, d = x.shape
    t = batch * seq
    x2 = x.reshape(t, d)
    row = lambda a: a.reshape(1, -1).astype(F32)

    tables = _rope_tables(positions)
    w1, wq, wk, wv = _proj_weights(w_in[0], w_q_b[0], w_kv_b[0])
    qa, ka, va, qb, kb, vb = _projections(x2, row(ln_emb_g), row(ln_emb_b), w1, wq, wk, wv,
                                          row(q_a_norm_g[0]), row(kv_a_norm_g[0]), tables)
    oa = _attention_a(qa, ka, va, a_sink[0].reshape(1, A_HEADS).astype(F32), batch, seq)
    ob = _attention_b(qb, kb, vb, batch, seq)

    pad_heads = lambda a: jnp.pad(a.reshape(B_HEADS, B_V, -1), ((0, 0), (0, LANES - B_V), (0, 0))).reshape(PAD_COLS, -1)
    wo = w_out[0]
    woa = wo[:A_Q_COLS].astype(BF16)
    wob = pad_heads(wo[A_Q_COLS:]).astype(BF16)
    gb = pad_heads(out_norm_b_g[0].reshape(-1, 1)).reshape(1, PAD_COLS).astype(F32)
    wr = jnp.concatenate([w_expert[0], w_group[0],
                          jnp.zeros((d, LANES - N_EXPERTS - N_GROUPS), F32)], -1)
    wr_hi = wr.astype(BF16)
    wr_lo = (wr - wr_hi.astype(F32)).astype(BF16)
    br = jnp.concatenate([b_expert[0], b_group[0], jnp.zeros((LANES - N_EXPERTS - N_GROUPS,), F32)]).reshape(1, LANES)
    consts = [row(out_norm_a_g[0]), gb, woa, wob, row(ln_emb_g), row(ln_emb_b), row(ln_attn_g[0]), row(ln_attn_b[0]),
              wr_hi, wr_lo, br]
    h1, h1p, route, counts = _outproj_router(oa, ob, x2, consts)

    n_tiles = (2 * t) // TE + N_EXPERTS
    slots, tmap = _slot_assign(route, counts, n_tiles)
    xs = _dispatch(_slot_blocks(slots, min(TM_DISP, t)), h1p, n_tiles * TE)
    y = _expert_mlp(tmap[:, 0], tmap[:, 1], xs, w_gate[0], w_up[0], w_down[0])
    out = _combine(_slot_blocks(slots, TM_COMB), h1, route, row(ln_ffn_g[0]), row(ln_ffn_b[0]), y)
    return out.reshape(batch, seq, d)
```

```python
import functools

import numpy as np
import jax
import jax.numpy as jnp
from jax import lax
from jax.experimental import pallas as pl
from jax.experimental.pallas import tpu as pltpu

F32 = jnp.float32
BF16 = jnp.bfloat16

D_MODEL = 1024
A_HEADS = 8
A_KV_HEADS = 2
A_HEAD_DIM = 64
WINDOW = 128
B_HEADS = 8
B_NOPE = 64
B_ROPE = 32
B_V = 64
B_Q_RANK = 768
B_KV_RANK = 256
ROPE_THETA = 10000.0
N_GROUPS = 4
EXPERTS_PER_GROUP = 8
N_EXPERTS = N_GROUPS * EXPERTS_PER_GROUP
EXPERT_FF = 256
LN_EPS = 1e-5
RMS_EPS = 1e-6
DEPTH = 1
ALPHA = (2 * DEPTH) ** 0.25
NEG_BIG = -1e30
LOG2_E = 1.4426950408889634

LANES = 128
A_Q_COLS = A_HEADS * A_HEAD_DIM
A_KV_COLS = A_KV_HEADS * A_HEAD_DIM
PAD_COLS = B_HEADS * LANES
BF16_SUBLANES = 16
VT_ROWS = B_V + BF16_SUBLANES

C_QA = 0
C_KA = C_QA + A_Q_COLS
C_VA = C_KA + 2 * A_KV_HEADS * LANES
C_CQ = C_VA + 2 * A_KV_HEADS * LANES
C_CKV = C_CQ + B_Q_RANK
C_KR = C_CKV + B_KV_RANK
W1_COLS = C_KR + LANES

TM_PROJ = 256
TQ_A = 256
TQ_B = 512
TK_B = 512
KT_PER_ITER = 4
TE = 256
TM_SLOT = 1024
TM_DISP = 512
TM_COMB = 256
VMEM_LIMIT = 56 * 1024 * 1024


def _cparams(sem):
    return pltpu.CompilerParams(dimension_semantics=sem, vmem_limit_bytes=VMEM_LIMIT)


def _layer_norm(x, g, b):
    mu = jnp.mean(x, -1, keepdims=True)
    xc = x - mu
    var = jnp.mean(xc * xc, -1, keepdims=True)
    return xc * lax.rsqrt(var + LN_EPS) * g + b


def _rope_block(xc, cos, sin_signed, first_half, half):
    rot = jnp.where(first_half, pltpu.roll(xc, LANES - half, 1), pltpu.roll(xc, half, 1))
    return xc * cos + rot * sin_signed


def _rope_table_body(pos_ref, inva_ref, sga_ref, invb_ref, sgb_ref, cosa_ref, sina_ref, cosb_ref, sinb_ref):
    pos = pos_ref[...].astype(F32)
    ang_a = pos * inva_ref[...]
    cosa_ref[...] = jnp.cos(ang_a)
    sina_ref[...] = jnp.sin(ang_a) * sga_ref[...]
    ang_b = pos * invb_ref[...]
    cosb_ref[...] = jnp.cos(ang_b)
    sinb_ref[...] = jnp.sin(ang_b) * sgb_ref[...]


def _rope_tables(positions):
    t = positions.shape[0] * positions.shape[1]
    tm = 512
    lane = np.arange(LANES)
    inv_a = 1.0 / (ROPE_THETA ** (np.arange(0, A_HEAD_DIM, 2, dtype=np.float64) / A_HEAD_DIM))
    inv_b = 1.0 / (ROPE_THETA ** (np.arange(0, B_ROPE, 2, dtype=np.float64) / B_ROPE))
    inva = inv_a[lane % (A_HEAD_DIM // 2)]
    sga = np.where(lane % A_HEAD_DIM < A_HEAD_DIM // 2, -1.0, 1.0)
    in_rope = (lane >= B_NOPE) & (lane < B_NOPE + B_ROPE)
    invb = np.where(in_rope, inv_b[(lane - B_NOPE) % (B_ROPE // 2)], 0.0)
    sgb = np.where(lane - B_NOPE < B_ROPE // 2, -1.0, 1.0)
    consts = [jnp.asarray(a.reshape(1, LANES), F32) for a in (inva, sga, invb, sgb)]
    row = pl.BlockSpec((1, LANES), lambda i: (0, 0))
    tab = pl.BlockSpec((tm, LANES), lambda i: (i, 0))
    return pl.pallas_call(
        _rope_table_body,
        grid=(t // tm,),
        in_specs=[pl.BlockSpec((tm, 1), lambda i: (i, 0)), row, row, row, row],
        out_specs=[tab, tab, tab, tab],
        out_shape=[jax.ShapeDtypeStruct((t, LANES), F32)] * 4,
        compiler_params=_cparams(("parallel",)),
        name="rope_tables",
    )(positions.reshape(t, 1), *consts)


def _proj_body(x_ref, g_ref, b_ref, w1_ref, wq_ref, wk_ref, wv_ref, gq_ref, gkv_ref,
               cosa_ref, sina_ref, cosb_ref, sinb_ref,
               qa_ref, ka_ref, va_ref, qb_ref, kb_ref, vt_ref):
    h = _layer_norm(x_ref[...], g_ref[...], b_ref[...]).astype(BF16)
    lane = lax.broadcasted_iota(jnp.int32, (1, LANES), 1)
    first_a = (lane % A_HEAD_DIM) < (A_HEAD_DIM // 2)
    first_b = (lane % B_ROPE) < (B_ROPE // 2)
    cosa, sina = cosa_ref[...], sina_ref[...]
    cosb, sinb = cosb_ref[...], sinb_ref[...]

    def proj(c0, width):
        return jnp.dot(h, w1_ref[:, c0:c0 + width], preferred_element_type=F32)

    qa = proj(C_QA, A_Q_COLS)
    for c in range(A_Q_COLS // LANES):
        blk = _rope_block(qa[:, c * LANES:(c + 1) * LANES], cosa, sina, first_a, A_HEAD_DIM // 2)
        qa_ref[:, c * LANES:(c + 1) * LANES] = (blk * (A_HEAD_DIM ** -0.5)).astype(BF16)
    ka = proj(C_KA, 2 * A_KV_HEADS * LANES)
    for c in range(2 * A_KV_HEADS):
        blk = _rope_block(ka[:, c * LANES:(c + 1) * LANES], cosa, sina, first_a, A_HEAD_DIM // 2)
        ka_ref[:, c * LANES:(c + 1) * LANES] = blk.astype(BF16)
    va_ref[...] = proj(C_VA, 2 * A_KV_HEADS * LANES).astype(BF16)

    cq = proj(C_CQ, B_Q_RANK)
    cq = cq * lax.rsqrt(jnp.mean(cq * cq, -1, keepdims=True) + RMS_EPS) * gq_ref[...]
    qb = jnp.dot(cq.astype(BF16), wq_ref[...], preferred_element_type=F32)
    scale_b = (B_NOPE + B_ROPE) ** -0.5 * LOG2_E
    for c in range(B_HEADS):
        blk = _rope_block(qb[:, c * LANES:(c + 1) * LANES], cosb, sinb, first_b, B_ROPE // 2)
        qb_ref[:, c * LANES:(c + 1) * LANES] = (blk * scale_b).astype(BF16)

    ckv = proj(C_CKV, B_KV_RANK)
    ckv = (ckv * lax.rsqrt(jnp.mean(ckv * ckv, -1, keepdims=True) + RMS_EPS) * gkv_ref[...]).astype(BF16)
    kr = _rope_block(proj(C_KR, LANES), cosb, sinb, first_b, B_ROPE // 2)
    kn = jnp.dot(ckv, wk_ref[...], preferred_element_type=F32)
    for c in range(B_HEADS):
        kb_ref[:, c * LANES:(c + 1) * LANES] = (kn[:, c * LANES:(c + 1) * LANES] + kr).astype(BF16)
    vt = lax.dot_general(wv_ref[...], ckv, (((1,), (1,)), ((), ())), preferred_element_type=F32)
    vrow = lax.broadcasted_iota(jnp.int32, (B_HEADS * VT_ROWS, 1), 0)
    vt_ref[...] = jnp.where(vrow % VT_ROWS >= B_V, 1.0, vt).astype(BF16)


def _lo_hi(w):
    d = w.shape[0]
    w = w.reshape(d, A_KV_HEADS, A_HEAD_DIM)
    z = jnp.zeros_like(w)
    return jnp.concatenate([w, z, z, w], -1).reshape(d, A_KV_HEADS * 2 * LANES)


def _proj_weights(w_in, w_q_b, w_kv_b):
    w = w_in
    d = w.shape[0]
    o = A_Q_COLS
    ka = w[:, o:o + A_KV_COLS]; o += A_KV_COLS
    va = w[:, o:o + A_KV_COLS]; o += A_KV_COLS
    cq = w[:, o:o + B_Q_RANK]; o += B_Q_RANK
    ckv = w[:, o:o + B_KV_RANK]; o += B_KV_RANK
    kr = w[:, o:o + B_ROPE]
    kr_blk = jnp.concatenate([jnp.zeros((d, B_NOPE), w.dtype), kr,
                              jnp.zeros((d, LANES - B_NOPE - B_ROPE), w.dtype)], -1)
    w1 = jnp.concatenate([w[:, :A_Q_COLS], _lo_hi(ka), _lo_hi(va), cq, ckv, kr_blk], -1).astype(BF16)
    wq = w_q_b.reshape(B_Q_RANK, B_HEADS, B_NOPE + B_ROPE)
    wq = jnp.pad(wq, ((0, 0), (0, 0), (0, LANES - B_NOPE - B_ROPE))).reshape(B_Q_RANK, PAD_COLS).astype(BF16)
    wkv = w_kv_b.reshape(B_KV_RANK, B_HEADS, B_NOPE + B_V)
    wk = jnp.pad(wkv[:, :, :B_NOPE], ((0, 0), (0, 0), (0, LANES - B_NOPE))).reshape(B_KV_RANK, PAD_COLS).astype(BF16)
    wv = jnp.pad(wkv[:, :, B_NOPE:], ((0, 0), (0, 0), (0, VT_ROWS - B_V)))
    wv = wv.reshape(B_KV_RANK, B_HEADS * VT_ROWS).T.astype(BF16)
    return w1, wq, wk, wv


def _projections(x2, ln_g, ln_b, w1, wq, wk, wv, gq, gkv, tables):
    t = x2.shape[0]
    tm = TM_PROJ
    full = lambda a: pl.BlockSpec(a.shape, lambda i: (0,) * a.ndim)
    tile = lambda n: pl.BlockSpec((tm, n), lambda i: (i, 0))
    consts = [ln_g, ln_b, w1, wq, wk, wv, gq, gkv]
    out_cols = [A_Q_COLS, 2 * A_KV_HEADS * LANES, 2 * A_KV_HEADS * LANES, PAD_COLS, PAD_COLS]
    vt_rows = B_HEADS * VT_ROWS
    return pl.pallas_call(
        _proj_body,
        grid=(t // tm,),
        in_specs=[tile(D_MODEL)] + [full(a) for a in consts] + [tile(LANES)] * 4,
        out_specs=[tile(n) for n in out_cols] + [pl.BlockSpec((vt_rows, tm), lambda i: (0, i))],
        out_shape=[jax.ShapeDtypeStruct((t, n), BF16) for n in out_cols]
        + [jax.ShapeDtypeStruct((vt_rows, t), BF16)],
        compiler_params=_cparams(("parallel",)),
        name="ln_in_proj",
    )(x2, *consts, *tables)


def _attn_a_body(seq, sink_ref, q_ref, k0, k1, k2, k3, v0, v1, v2, v3, o_ref):
    i = pl.program_id(1)
    q = q_ref[...]
    kk = jnp.concatenate([k0[...], k1[...], k2[...], k3[...]], 0)
    vv = jnp.concatenate([v0[...], v1[...], v2[...], v3[...]], 0)
    nk = kk.shape[0]
    qpos = i * TQ_A + lax.broadcasted_iota(jnp.int32, (TQ_A, 1), 0)
    kpos = i * TQ_A - WINDOW + lax.broadcasted_iota(jnp.int32, (1, nk), 1)
    valid = (kpos >= 0) & (kpos < seq) & (jnp.abs(kpos - qpos) <= WINDOW)
    for pair in range(A_HEADS // 2):
        g = (2 * pair) // (A_HEADS // A_KV_HEADS)
        qp = q[:, pair * LANES:(pair + 1) * LANES]
        o = jnp.zeros((TQ_A, LANES), F32)
        for par in range(2):
            c = (2 * g + par) * LANES
            sink = sink_ref[0, 2 * pair + par]
            s = lax.dot_general(qp, kk[:, c:c + LANES], (((1,), (1,)), ((), ())), preferred_element_type=F32)
            s = jnp.where(valid, s, NEG_BIG)
            m = jnp.maximum(jnp.max(s, -1, keepdims=True), sink)
            e = jnp.exp(s - m)
            den = jnp.sum(e, -1, keepdims=True) + jnp.exp(sink - m)
            p = (e / den).astype(BF16)
            o = o + jnp.dot(p, vv[:, c:c + LANES], preferred_element_type=F32)
        o_ref[:, pair * LANES:(pair + 1) * LANES] = o


def _attention_a(qa, ka, va, sink, batch, seq):
    t = qa.shape[0]
    nq = seq // TQ_A
    nb = seq // WINDOW
    per = TQ_A // WINDOW

    def kv_spec(j):
        def imap(b, i):
            return (b * nb + jnp.clip(per * i - 1 + j, 0, nb - 1), 0)
        return pl.BlockSpec((WINDOW, 2 * A_KV_HEADS * LANES), imap)

    kv_specs = [kv_spec(j) for j in range(per + 2)]
    return pl.pallas_call(
        functools.partial(_attn_a_body, seq),
        grid=(batch, nq),
        in_specs=[pl.BlockSpec(memory_space=pltpu.SMEM),
                  pl.BlockSpec((TQ_A, A_Q_COLS), lambda b, i: (b * nq + i, 0))] + kv_specs + kv_specs,
        out_specs=pl.BlockSpec((TQ_A, A_Q_COLS), lambda b, i: (b * nq + i, 0)),
        out_shape=jax.ShapeDtypeStruct((t, A_Q_COLS), F32),
        compiler_params=_cparams(("parallel", "parallel")),
        name="window_attn",
    )(sink, qa, *([ka] * (per + 2)), *([va] * (per + 2)))


def _attn_b_body(nkt, q_ref, k_ref, vt_ref, o_ref, s_sc, acc_sc):
    qt = q_ref[...].astype(F32).T.astype(BF16)
    acc_sc[...] = jnp.zeros(acc_sc.shape, F32)

    def scores(j):
        start = pl.multiple_of(j * TK_B, TK_B)
        s = jnp.dot(k_ref[pl.ds(start, TK_B), :], qt, preferred_element_type=F32)
        return s, jnp.max(s, 0, keepdims=True)

    def update(j, s, m, m_tile):
        start = pl.multiple_of(j * TK_B, TK_B)
        m_new = jnp.maximum(m, m_tile)
        alpha = jnp.exp2(m - m_new)
        p = jnp.exp2(s - m_new).astype(BF16)
        pv = jnp.dot(vt_ref[:, pl.ds(start, TK_B)], p, preferred_element_type=F32)
        acc_sc[...] = alpha * acc_sc[...] + pv
        return m_new

    s0, mt0 = scores(0)
    s_sc[0] = s0

    def group(jj, carry):
        m, m_tile = carry
        for u in range(KT_PER_ITER):
            j = KT_PER_ITER * jj + u
            s_next, mt_next = scores(jnp.minimum(j + 1, nkt - 1))
            s_sc[(u + 1) % 2] = s_next
            m = update(j, s_sc[u % 2], m, m_tile)
            m_tile = mt_next
        return m, m_tile

    m0 = jnp.full((1, TQ_B), -jnp.inf, F32)
    lax.fori_loop(0, nkt // KT_PER_ITER, group, (m0, mt0))
    o_ref[...] = acc_sc[:B_V, :] / acc_sc[B_V:B_V + 1, :]


def _attention_b(qb, kb, vt, batch, seq):
    t = qb.shape[0]
    nq = seq // TQ_B
    nkt = seq // TK_B
    assert nkt % KT_PER_ITER == 0 and KT_PER_ITER % 2 == 0
    return pl.pallas_call(
        functools.partial(_attn_b_body, nkt),
        grid=(batch * B_HEADS, nq),
        in_specs=[pl.BlockSpec((TQ_B, LANES), lambda bh, i: ((bh // B_HEADS) * nq + i, bh % B_HEADS)),
                  pl.BlockSpec((seq, LANES), lambda bh, i: (bh // B_HEADS, bh % B_HEADS)),
                  pl.BlockSpec((VT_ROWS, seq), lambda bh, i: (bh % B_HEADS, bh // B_HEADS))],
        out_specs=pl.BlockSpec((B_V, TQ_B), lambda bh, i: (bh % B_HEADS, (bh // B_HEADS) * nq + i)),
        out_shape=jax.ShapeDtypeStruct((B_HEADS * B_V, t), F32),
        scratch_shapes=[pltpu.VMEM((2, TK_B, TQ_B), F32), pltpu.VMEM((VT_ROWS, TQ_B), F32)],
        compiler_params=_cparams(("parallel", "arbitrary")),
        name="mla_attn",
    )(qb, kb, vt)


R_E1, R_E2, R_RANK1, R_RANK2, R_W1, R_W2 = range(6)
GROUP_LANE0 = N_EXPERTS


def _lane_pick(slab, lane, idx):
    return jnp.sum(jnp.where(lane == idx, slab, 0.0), -1, keepdims=True)


def _outproj_body(oa_ref, ob_ref, x_ref, ga_ref, gb_ref, woa_ref, wob_ref, lg0_ref, lb0_ref, lg1_ref, lb1_ref,
                  wrh_ref, wrl_ref, br_ref, h1_ref, h1p_ref, route_ref, counts_ref, carry_sc):
    @pl.when(pl.program_id(0) == 0)
    def _():
        carry_sc[...] = jnp.zeros(carry_sc.shape, F32)

    oa = oa_ref[...]
    ob = ob_ref[...].T
    na = oa * lax.rsqrt(jnp.mean(oa * oa, -1, keepdims=True) + RMS_EPS) * ga_ref[...]
    nb = ob * lax.rsqrt(jnp.mean(ob * ob, -1, keepdims=True) + RMS_EPS) * gb_ref[...]
    mixed = (jnp.dot(na.astype(BF16), woa_ref[...], preferred_element_type=F32)
             + jnp.dot(nb.astype(BF16), wob_ref[...], preferred_element_type=F32))
    h0 = _layer_norm(x_ref[...], lg0_ref[...], lb0_ref[...])
    h1 = _layer_norm(ALPHA * h0 + mixed, lg1_ref[...], lb1_ref[...])
    h1_ref[...] = h1

    hi = h1.astype(BF16)
    hi32 = lax.bitcast_convert_type(hi.astype(F32), jnp.uint32)
    half = D_MODEL // 2
    h1p_ref[...] = hi32[:, :half] | (hi32[:, half:] >> 16)

    lo = (h1 - hi.astype(F32)).astype(BF16)
    logits = (jnp.dot(hi, wrh_ref[...], preferred_element_type=F32)
              + jnp.dot(lo, wrh_ref[...], preferred_element_type=F32)
              + jnp.dot(hi, wrl_ref[...], preferred_element_type=F32)) + br_ref[...]

    tm = logits.shape[0]
    lane = lax.broadcasted_iota(jnp.int32, (1, LANES), 1).astype(F32)
    far = float(LANES)
    is_g = (lane >= GROUP_LANE0) & (lane < GROUP_LANE0 + N_GROUPS)
    gl = jnp.where(is_g, logits, -jnp.inf)
    gmax = jnp.max(gl, -1, keepdims=True)
    gidx = jnp.min(jnp.where(gl == gmax, lane, far), -1, keepdims=True) - GROUP_LANE0
    pg = 1.0 / jnp.sum(jnp.where(is_g, jnp.exp(jnp.where(is_g, logits, gmax) - gmax), 0.0), -1, keepdims=True)
    in_grp = (lane >= gidx * EXPERTS_PER_GROUP) & (lane < (gidx + 1.0) * EXPERTS_PER_GROUP)
    el = jnp.where(in_grp, logits, -jnp.inf)
    v1 = jnp.max(el, -1, keepdims=True)
    i1 = jnp.min(jnp.where(el == v1, lane, far), -1, keepdims=True)
    el2 = jnp.where(lane == i1, -jnp.inf, el)
    v2 = jnp.max(el2, -1, keepdims=True)
    i2 = jnp.min(jnp.where(el2 == v2, lane, far), -1, keepdims=True)
    ex = jnp.exp(v2 - v1)
    w1 = pg / (1.0 + ex)
    w2 = pg * ex / (1.0 + ex)

    onehot = jnp.where((lane == i1) | (lane == i2), 1.0, 0.0)
    r_i = lax.broadcasted_iota(jnp.int32, (tm, tm), 0)
    c_i = lax.broadcasted_iota(jnp.int32, (tm, tm), 1)
    lower = jnp.where(c_i < r_i, 1.0, 0.0).astype(BF16)
    prefix = jnp.dot(lower, onehot.astype(BF16), preferred_element_type=F32) + carry_sc[0:1, :]
    rank1 = _lane_pick(prefix, lane, i1)
    rank2 = _lane_pick(prefix, lane, i2)
    new_carry = carry_sc[0:1, :] + jnp.sum(onehot, 0, keepdims=True)
    carry_sc[...] = jnp.broadcast_to(new_carry, carry_sc.shape)
    counts_ref[...] = jnp.broadcast_to(new_carry, counts_ref.shape)

    slab = jnp.zeros((tm, LANES), F32)
    for idx, val in ((R_E1, i1), (R_E2, i2), (R_RANK1, rank1), (R_RANK2, rank2), (R_W1, w1), (R_W2, w2)):
        slab = jnp.where(lane == idx, val, slab)
    route_ref[...] = slab


def _outproj_router(oa, ob, x2, consts):
    t = x2.shape[0]
    tm = TM_PROJ
    full = lambda a: pl.BlockSpec(a.shape, lambda i: (0,) * a.ndim)
    tile = lambda n: pl.BlockSpec((tm, n), lambda i: (i, 0))
    return pl.pallas_call(
        _outproj_body,
        grid=(t // tm,),
        in_specs=[tile(A_Q_COLS), pl.BlockSpec((B_HEADS * B_V, tm), lambda i: (0, i)), tile(D_MODEL)]
        + [full(a) for a in consts],
        out_specs=[tile(D_MODEL), tile(D_MODEL // 2), tile(LANES), pl.BlockSpec((8, LANES), lambda i: (0, 0))],
        out_shape=[jax.ShapeDtypeStruct((t, D_MODEL), F32), jax.ShapeDtypeStruct((t, D_MODEL // 2), jnp.uint32),
                   jax.ShapeDtypeStruct((t, LANES), F32), jax.ShapeDtypeStruct((8, LANES), F32)],
        scratch_shapes=[pltpu.VMEM((8, LANES), F32)],
        compiler_params=_cparams(("arbitrary",)),
        name="out_proj_router",
    )(oa, ob, x2, *consts)


def _slot_body(n_tiles, route_ref, counts_ref, slots_ref, tmap_ref):
    lane_i = lax.broadcasted_iota(jnp.int32, (1, LANES), 1)
    lane = lane_i.astype(F32)
    cnt = jnp.where(lane_i < N_EXPERTS, counts_ref[0:1, :], 0.0)
    padded = jnp.floor((cnt + (TE - 1)) * (1.0 / TE)) * TE
    incl = padded
    for k in (1, 2, 4, 8, 16):
        incl = incl + jnp.where(lane_i >= k, pltpu.roll(incl, k, 1), 0.0)
    off = incl - padded

    route = route_ref[...]
    e1 = _lane_pick(route, lane, R_E1)
    e2 = _lane_pick(route, lane, R_E2)
    s1 = _lane_pick(jnp.broadcast_to(off, route.shape), lane, e1) + _lane_pick(route, lane, R_RANK1)
    s2 = _lane_pick(jnp.broadcast_to(off, route.shape), lane, e2) + _lane_pick(route, lane, R_RANK2)
    slab = jnp.where(lane == 0, s1, jnp.where(lane == 1, s2, 0.0))
    slots_ref[...] = slab.astype(jnp.int32)

    row0 = lax.broadcasted_iota(jnp.int32, (n_tiles, 1), 0).astype(F32) * TE
    is_e = lane_i < N_EXPERTS
    te = jnp.sum(jnp.where(is_e & (incl <= row0), 1.0, 0.0), -1, keepdims=True)
    te = jnp.minimum(te, N_EXPERTS - 1.0)
    used = jnp.sum(jnp.where(lane == te, cnt - (row0 - off), 0.0), -1, keepdims=True)
    used = jnp.clip(used, 0.0, float(TE))
    tmap = jnp.where(lane == 0, te, jnp.where(lane == 1, used, 0.0))
    tmap_ref[...] = tmap.astype(jnp.int32)


def _slot_assign(route, counts, n_tiles):
    t = route.shape[0]
    tm = min(TM_SLOT, t)
    return pl.pallas_call(
        functools.partial(_slot_body, n_tiles),
        grid=(t // tm,),
        in_specs=[pl.BlockSpec((tm, LANES), lambda i: (i, 0)), pl.BlockSpec((8, LANES), lambda i: (0, 0))],
        out_specs=[pl.BlockSpec((tm, LANES), lambda i: (i, 0)), pl.BlockSpec((n_tiles, LANES), lambda i: (0, 0))],
        out_shape=[jax.ShapeDtypeStruct((t, LANES), jnp.int32), jax.ShapeDtypeStruct((n_tiles, LANES), jnp.int32)],
        compiler_params=_cparams(("arbitrary",)),
        name="slot_assign",
    )(route, counts)


def _dispatch_body(tm, slots_ref, src_ref, zeros_ref, dst_ref, sem):
    del zeros_ref

    def row_copy(j, k):
        return pltpu.make_async_copy(src_ref.at[pl.ds(j, 1), :],
                                     dst_ref.at[pl.ds(slots_ref[0, k, j], 1), :], sem)

    def start(j, carry):
        row_copy(j, 0).start(priority=0)
        row_copy(j, 1).start(priority=1)
        return carry

    def wait(j, carry):
        row_copy(j, 0).wait()
        row_copy(j, 1).wait()
        return carry

    lax.fori_loop(0, tm, start, 0, unroll=8)
    lax.fori_loop(0, tm, wait, 0, unroll=8)


def _dispatch(slots3, h1p, n_rows):
    t = h1p.shape[0]
    tm = slots3.shape[2]
    return pl.pallas_call(
        functools.partial(_dispatch_body, tm),
        grid=(t // tm,),
        in_specs=[pl.BlockSpec((1, 2, tm), lambda i: (i, 0, 0), memory_space=pltpu.SMEM),
                  pl.BlockSpec((tm, h1p.shape[1]), lambda i: (i, 0)), pl.BlockSpec(memory_space=pl.ANY)],
        out_specs=pl.BlockSpec(memory_space=pl.ANY),
        out_shape=jax.ShapeDtypeStruct((n_rows, h1p.shape[1]), h1p.dtype),
        scratch_shapes=[pltpu.SemaphoreType.DMA(())],
        input_output_aliases={2: 0},
        compiler_params=_cparams(("arbitrary",)),
        name="moe_dispatch",
    )(slots3, h1p, jnp.zeros((n_rows, h1p.shape[1]), h1p.dtype))


def _expert_body(te_ref, used_ref, xs_ref, wg_ref, wu_ref, wd_ref, y_ref, wg_sc, wu_sc, wd_sc):
    r = pl.program_id(0)
    used = used_ref[r]
    prev = te_ref[jnp.maximum(r - 1, 0)]

    @pl.when((r == 0) | (te_ref[r] != prev))
    def _():
        wg_sc[...] = wg_ref[0].astype(BF16)
        wu_sc[...] = wu_ref[0].astype(BF16)
        wd_sc[...] = wd_ref[0].astype(BF16)

    @pl.when(used > 0)
    def _():
        words = xs_ref[...]
        xa = lax.bitcast_convert_type(words & jnp.uint32(0xFFFF0000), F32).astype(BF16)
        xb = lax.bitcast_convert_type(words << 16, F32).astype(BF16)
        half = D_MODEL // 2
        g = (jnp.dot(xa, wg_sc[:half, :], preferred_element_type=F32)
             + jnp.dot(xb, wg_sc[half:, :], preferred_element_type=F32))
        u = (jnp.dot(xa, wu_sc[:half, :], preferred_element_type=F32)
             + jnp.dot(xb, wu_sc[half:, :], preferred_element_type=F32))
        hid = (g * (1.0 / (1.0 + jnp.exp(-g))) * u).astype(BF16)
        y_ref[...] = jnp.dot(hid, wd_sc[...], preferred_element_type=F32)

    @pl.when(used <= 0)
    def _():
        y_ref[...] = jnp.zeros(y_ref.shape, F32)


def _expert_mlp(tile_expert, tile_used, xs, w_gate, w_up, w_down):
    n_rows = xs.shape[0]
    n_tiles = n_rows // TE
    grid_spec = pltpu.PrefetchScalarGridSpec(
        num_scalar_prefetch=2,
        grid=(n_tiles,),
        in_specs=[pl.BlockSpec((TE, D_MODEL // 2), lambda r, te, us: (r, 0)),
                  pl.BlockSpec((1, D_MODEL, EXPERT_FF), lambda r, te, us: (te[r], 0, 0)),
                  pl.BlockSpec((1, D_MODEL, EXPERT_FF), lambda r, te, us: (te[r], 0, 0)),
                  pl.BlockSpec((1, EXPERT_FF, D_MODEL), lambda r, te, us: (te[r], 0, 0))],
        out_specs=pl.BlockSpec((TE, D_MODEL), lambda r, te, us: (r, 0)),
        scratch_shapes=[pltpu.VMEM((D_MODEL, EXPERT_FF), BF16), pltpu.VMEM((D_MODEL, EXPERT_FF), BF16),
                        pltpu.VMEM((EXPERT_FF, D_MODEL), BF16)],
    )
    return pl.pallas_call(
        _expert_body,
        grid_spec=grid_spec,
        out_shape=jax.ShapeDtypeStruct((n_rows, D_MODEL), F32),
        compiler_params=_cparams(("arbitrary",)),
        name="expert_mlp",
    )(tile_expert, tile_used, xs, w_gate, w_up, w_down)


def _combine_body(tm, slots_ref, h1_ref, route_ref, g_ref, b_ref, y_ref, o_ref, ybuf, sem):
    def row_copy(j, k):
        return pltpu.make_async_copy(y_ref.at[pl.ds(slots_ref[0, k, j], 1), :],
                                     ybuf.at[k, pl.ds(j, 1), :], sem)

    def start(j, carry):
        row_copy(j, 0).start(priority=0)
        row_copy(j, 1).start(priority=1)
        return carry

    def wait(j, carry):
        row_copy(j, 0).wait()
        row_copy(j, 1).wait()
        return carry

    lax.fori_loop(0, tm, start, 0, unroll=8)
    lax.fori_loop(0, tm, wait, 0, unroll=8)

    lane = lax.broadcasted_iota(jnp.int32, (1, LANES), 1).astype(F32)
    route = route_ref[...]
    w1 = _lane_pick(route, lane, R_W1)
    w2 = _lane_pick(route, lane, R_W2)
    ffn = w1 * ybuf[0] + w2 * ybuf[1]
    o_ref[...] = _layer_norm(ALPHA * h1_ref[...] + ffn, g_ref[...], b_ref[...])


def _combine(slots3, h1, route, ln_g, ln_b, y):
    t = h1.shape[0]
    tm = slots3.shape[2]
    return pl.pallas_call(
        functools.partial(_combine_body, tm),
        grid=(t // tm,),
        in_specs=[pl.BlockSpec((1, 2, tm), lambda i: (i, 0, 0), memory_space=pltpu.SMEM),
                  pl.BlockSpec((tm, D_MODEL), lambda i: (i, 0)),
                  pl.BlockSpec((tm, LANES), lambda i: (i, 0)),
                  pl.BlockSpec((1, D_MODEL), lambda i: (0, 0)),
                  pl.BlockSpec((1, D_MODEL), lambda i: (0, 0)),
                  pl.BlockSpec(memory_space=pl.ANY)],
        out_specs=pl.BlockSpec((tm, D_MODEL), lambda i: (i, 0)),
        out_shape=jax.ShapeDtypeStruct((t, D_MODEL), F32),
        scratch_shapes=[pltpu.VMEM((2, tm, D_MODEL), F32), pltpu.SemaphoreType.DMA(())],
        compiler_params=_cparams(("arbitrary",)),
        name="moe_combine",
    )(slots3, h1, route, ln_g, ln_b, y)


def _slot_blocks(slots, tm):
    t = slots.shape[0]
    return slots[:, :2].reshape(t // tm, tm, 2).transpose(0, 2, 1)


def kernel(x, positions, ln_emb_g, ln_emb_b, w_in, a_sink, q_a_norm_g, w_q_b, kv_a_norm_g, w_kv_b, out_norm_a_g, out_norm_b_g, w_out, ln_attn_g, ln_attn_b, w_group, b_group, w_expert, b_expert, w_gate, w_up, w_down, ln_ffn_g, ln_ffn_b):
    batch, seq, d = x.shape
    t = batch * seq
    x2 = x.reshape(t, d)
    row = lambda a: a.reshape(1, -1).astype(F32)

    tables = _rope_tables(positions)
    w1, wq, wk, wv = _proj_weights(w_in[0], w_q_b[0], w_kv_b[0])
    qa, ka, va, qb, kb, vt = _projections(x2, row(ln_emb_g), row(ln_emb_b), w1, wq, wk, wv,
                                          row(q_a_norm_g[0]), row(kv_a_norm_g[0]), tables)
    oa = _attention_a(qa, ka, va, a_sink[0].reshape(1, A_HEADS).astype(F32), batch, seq)
    ob = _attention_b(qb, kb, vt, batch, seq)

    wo = w_out[0]
    woa = wo[:A_Q_COLS].astype(BF16)
    wob = wo[A_Q_COLS:].astype(BF16)
    gb = row(out_norm_b_g[0])
    wr = jnp.concatenate([w_expert[0], w_group[0],
                          jnp.zeros((d, LANES - N_EXPERTS - N_GROUPS), F32)], -1)
    wr_hi = wr.astype(BF16)
    wr_lo = (wr - wr_hi.astype(F32)).astype(BF16)
    br = jnp.concatenate([b_expert[0], b_group[0], jnp.zeros((LANES - N_EXPERTS - N_GROUPS,), F32)]).reshape(1, LANES)
    consts = [row(out_norm_a_g[0]), gb, woa, wob, row(ln_emb_g), row(ln_emb_b), row(ln_attn_g[0]), row(ln_attn_b[0]),
              wr_hi, wr_lo, br]
    h1, h1p, route, counts = _outproj_router(oa, ob, x2, consts)

    n_tiles = (2 * t) // TE + N_EXPERTS
    slots, tmap = _slot_assign(route, counts, n_tiles)
    xs = _dispatch(_slot_blocks(slots, min(TM_DISP, t)), h1p, n_tiles * TE)
    y = _expert_mlp(tmap[:, 0], tmap[:, 1], xs, w_gate[0], w_up[0], w_down[0])
    out = _combine(_slot_blocks(slots, TM_COMB), h1, route, row(ln_ffn_g[0]), row(ln_ffn_b[0]), y)
    return out.reshape(batch, seq, d)
```

```python
import functools

import numpy as np
import jax
import jax.numpy as jnp
from jax import lax
from jax.experimental import pallas as pl
from jax.experimental.pallas import tpu as pltpu

F32 = jnp.float32
BF16 = jnp.bfloat16

D_MODEL = 1024
A_HEADS = 8
A_KV_HEADS = 2
A_HEAD_DIM = 64
WINDOW = 128
B_HEADS = 8
B_NOPE = 64
B_ROPE = 32
B_V = 64
B_Q_RANK = 768
B_KV_RANK = 256
ROPE_THETA = 10000.0
N_GROUPS = 4
EXPERTS_PER_GROUP = 8
N_EXPERTS = N_GROUPS * EXPERTS_PER_GROUP
EXPERT_FF = 256
LN_EPS = 1e-5
RMS_EPS = 1e-6
DEPTH = 1
ALPHA = (2 * DEPTH) ** 0.25
NEG_BIG = -1e30
LOG2_E = 1.4426950408889634

LANES = 128
A_Q_COLS = A_HEADS * A_HEAD_DIM
A_KV_COLS = A_KV_HEADS * A_HEAD_DIM
PAD_COLS = B_HEADS * LANES
BF16_SUBLANES = 16
VT_ROWS = B_V + BF16_SUBLANES
A_GROUP = A_HEADS // A_KV_HEADS

C_QA = 0
C_KA = C_QA + A_HEADS * LANES
C_CQ = C_KA + A_KV_COLS
C_CKV = C_CQ + B_Q_RANK
C_KR = C_CKV + B_KV_RANK
W1_COLS = C_KR + LANES

TM_PROJ = 256
TQ_A = 256
TQ_B = 512
TK_B = 512
KT_PER_ITER = 16
TE = 256
TM_SLOT = 1024
TM_DISP = 512
TM_COMB = 256
VMEM_LIMIT = 56 * 1024 * 1024


def _cparams(sem):
    return pltpu.CompilerParams(dimension_semantics=sem, vmem_limit_bytes=VMEM_LIMIT)


def _layer_norm(x, g, b):
    mu = jnp.mean(x, -1, keepdims=True)
    xc = x - mu
    var = jnp.mean(xc * xc, -1, keepdims=True)
    return xc * lax.rsqrt(var + LN_EPS) * g + b


def _rope_block(xc, cos, sin_signed, first_half, half):
    rot = jnp.where(first_half, pltpu.roll(xc, LANES - half, 1), pltpu.roll(xc, half, 1))
    return xc * cos + rot * sin_signed


HALF_A = A_HEAD_DIM // 2
HALF_B = B_ROPE // 2


def _rope_table_body(pos_ref, inv_ref, cosa_ref, sina_ref, cosb_ref, sinb_ref):
    ang = pos_ref[...].astype(F32) * inv_ref[...]
    cc = jnp.cos(ang)
    ss = jnp.sin(ang)
    lane = lax.broadcasted_iota(jnp.int32, (1, LANES), 1)

    def table_a(x):
        out = x
        for k in range(1, LANES // HALF_A):
            out = jnp.where(lane >= k * HALF_A, pltpu.roll(x, k * HALF_A, 1), out)
        return out

    def table_b(x, fill):
        lo = pltpu.roll(x, B_NOPE - HALF_A, 1)
        hi = pltpu.roll(x, B_NOPE - HALF_A + HALF_B, 1)
        out = jnp.where(lane < B_NOPE + HALF_B, lo, hi)
        return jnp.where((lane >= B_NOPE) & (lane < B_NOPE + B_ROPE), out, fill)

    sign_a = jnp.where(lane % A_HEAD_DIM < HALF_A, -1.0, 1.0)
    sign_b = jnp.where(lane < B_NOPE + HALF_B, -1.0, 1.0)
    cosa_ref[...] = table_a(cc)
    sina_ref[...] = table_a(ss) * sign_a
    cosb_ref[...] = table_b(cc, 1.0)
    sinb_ref[...] = table_b(ss, 0.0) * sign_b


def _rope_tables(positions):
    t = positions.shape[0] * positions.shape[1]
    tm = 512
    inv_a = 1.0 / (ROPE_THETA ** (np.arange(0, A_HEAD_DIM, 2, dtype=np.float64) / A_HEAD_DIM))
    inv_b = 1.0 / (ROPE_THETA ** (np.arange(0, B_ROPE, 2, dtype=np.float64) / B_ROPE))
    inv = np.zeros((1, LANES))
    inv[0, :HALF_A] = inv_a
    inv[0, HALF_A:HALF_A + HALF_B] = inv_b
    tab = pl.BlockSpec((tm, LANES), lambda i: (i, 0))
    return pl.pallas_call(
        _rope_table_body,
        grid=(t // tm,),
        in_specs=[pl.BlockSpec((tm, 1), lambda i: (i, 0)), pl.BlockSpec((1, LANES), lambda i: (0, 0))],
        out_specs=[tab, tab, tab, tab],
        out_shape=[jax.ShapeDtypeStruct((t, LANES), F32)] * 4,
        compiler_params=_cparams(("parallel",)),
        name="rope_tables",
    )(positions.reshape(t, 1), jnp.asarray(inv, F32))


def _ones_rows(vt):
    vrow = lax.broadcasted_iota(jnp.int32, (vt.shape[0], 1), 0)
    return jnp.where(vrow % VT_ROWS >= B_V, 1.0, vt)


def _proj_body(x_ref, g_ref, b_ref, w1_ref, wva_ref, wq_ref, wk_ref, wv_ref, gq_ref, gkv_ref,
               cosa_ref, sina_ref, cosb_ref, sinb_ref,
               qa_ref, ka_ref, vat_ref, qb_ref, kb_ref, vt_ref):
    h = _layer_norm(x_ref[...], g_ref[...], b_ref[...]).astype(BF16)
    lane = lax.broadcasted_iota(jnp.int32, (1, LANES), 1)
    first_a = (lane % A_HEAD_DIM) < (A_HEAD_DIM // 2)
    first_b = (lane % B_ROPE) < (B_ROPE // 2)
    cosa, sina = cosa_ref[...], sina_ref[...]
    cosb, sinb = cosb_ref[...], sinb_ref[...]

    def proj(c0, width):
        return jnp.dot(h, w1_ref[:, c0:c0 + width], preferred_element_type=F32)

    qa = proj(C_QA, A_HEADS * LANES)
    scale_a = A_HEAD_DIM ** -0.5 * LOG2_E
    for c in range(A_HEADS):
        blk = _rope_block(qa[:, c * LANES:(c + 1) * LANES], cosa, sina, first_a, HALF_A)
        qa_ref[:, c * LANES:(c + 1) * LANES] = (blk * scale_a).astype(BF16)
    ka_ref[...] = _rope_block(proj(C_KA, A_KV_COLS), cosa, sina, first_a, HALF_A).astype(BF16)
    vat = lax.dot_general(wva_ref[...], h, (((1,), (1,)), ((), ())), preferred_element_type=F32)
    vat_ref[...] = _ones_rows(vat).astype(BF16)

    cq = proj(C_CQ, B_Q_RANK)
    cq = cq * lax.rsqrt(jnp.mean(cq * cq, -1, keepdims=True) + RMS_EPS) * gq_ref[...]
    qb = jnp.dot(cq.astype(BF16), wq_ref[...], preferred_element_type=F32)
    scale_b = (B_NOPE + B_ROPE) ** -0.5 * LOG2_E
    for c in range(B_HEADS):
        blk = _rope_block(qb[:, c * LANES:(c + 1) * LANES], cosb, sinb, first_b, HALF_B)
        qb_ref[:, c * LANES:(c + 1) * LANES] = (blk * scale_b).astype(BF16)

    ckv = proj(C_CKV, B_KV_RANK)
    ckv = (ckv * lax.rsqrt(jnp.mean(ckv * ckv, -1, keepdims=True) + RMS_EPS) * gkv_ref[...]).astype(BF16)
    kr = _rope_block(proj(C_KR, LANES), cosb, sinb, first_b, HALF_B)
    kn = jnp.dot(ckv, wk_ref[...], preferred_element_type=F32)
    for c in range(B_HEADS):
        kb_ref[:, c * LANES:(c + 1) * LANES] = (kn[:, c * LANES:(c + 1) * LANES] + kr).astype(BF16)
    vt = lax.dot_general(wv_ref[...], ckv, (((1,), (1,)), ((), ())), preferred_element_type=F32)
    vt_ref[...] = _ones_rows(vt).astype(BF16)


def _proj_weights(w_in, w_q_b, w_kv_b):
    w = w_in
    d = w.shape[0]
    o = A_Q_COLS
    ka = w[:, o:o + A_KV_COLS]; o += A_KV_COLS
    va = w[:, o:o + A_KV_COLS]; o += A_KV_COLS
    cq = w[:, o:o + B_Q_RANK]; o += B_Q_RANK
    ckv = w[:, o:o + B_KV_RANK]; o += B_KV_RANK
    kr = w[:, o:o + B_ROPE]
    kr_blk = jnp.concatenate([jnp.zeros((d, B_NOPE), w.dtype), kr,
                              jnp.zeros((d, LANES - B_NOPE - B_ROPE), w.dtype)], -1)
    qa = w[:, :A_Q_COLS].reshape(d, A_KV_HEADS, A_GROUP, A_HEAD_DIM)
    qa = jnp.stack([jnp.pad(qa[:, g], ((0, 0), (0, 0), (g * A_HEAD_DIM, LANES - (g + 1) * A_HEAD_DIM)))
                    for g in range(A_KV_HEADS)], 1).reshape(d, A_HEADS * LANES)
    w1 = jnp.concatenate([qa, ka, cq, ckv, kr_blk], -1).astype(BF16)
    wva = jnp.pad(va.reshape(d, A_KV_HEADS, A_HEAD_DIM), ((0, 0), (0, 0), (0, VT_ROWS - A_HEAD_DIM)))
    wva = wva.reshape(d, A_KV_HEADS * VT_ROWS).T.astype(BF16)
    wq = w_q_b.reshape(B_Q_RANK, B_HEADS, B_NOPE + B_ROPE)
    wq = jnp.pad(wq, ((0, 0), (0, 0), (0, LANES - B_NOPE - B_ROPE))).reshape(B_Q_RANK, PAD_COLS).astype(BF16)
    wkv = w_kv_b.reshape(B_KV_RANK, B_HEADS, B_NOPE + B_V)
    wk = jnp.pad(wkv[:, :, :B_NOPE], ((0, 0), (0, 0), (0, LANES - B_NOPE))).reshape(B_KV_RANK, PAD_COLS).astype(BF16)
    wv = jnp.pad(wkv[:, :, B_NOPE:], ((0, 0), (0, 0), (0, VT_ROWS - B_V)))
    wv = wv.reshape(B_KV_RANK, B_HEADS * VT_ROWS).T.astype(BF16)
    return w1, wva, wq, wk, wv


def _projections(x2, ln_g, ln_b, w1, wva, wq, wk, wv, gq, gkv, tables):
    t = x2.shape[0]
    tm = TM_PROJ
    full = lambda a: pl.BlockSpec(a.shape, lambda i: (0,) * a.ndim)
    tile = lambda n: pl.BlockSpec((tm, n), lambda i: (i, 0))
    consts = [ln_g, ln_b, w1, wva, wq, wk, wv, gq, gkv]
    tok = lambda n: (tile(n), jax.ShapeDtypeStruct((t, n), BF16))
    feat = lambda r: (pl.BlockSpec((r, tm), lambda i: (0, i)), jax.ShapeDtypeStruct((r, t), BF16))
    outs = [tok(A_HEADS * LANES), tok(A_KV_COLS), feat(A_KV_HEADS * VT_ROWS),
            tok(PAD_COLS), tok(PAD_COLS), feat(B_HEADS * VT_ROWS)]
    return pl.pallas_call(
        _proj_body,
        grid=(t // tm,),
        in_specs=[tile(D_MODEL)] + [full(a) for a in consts] + [tile(LANES)] * 4,
        out_specs=[o[0] for o in outs],
        out_shape=[o[1] for o in outs],
        compiler_params=_cparams(("parallel",)),
        name="ln_in_proj",
    )(x2, *consts, *tables)


def _attn_a_body(seq, sink_ref, q_ref, k0, k1, k2, k3, v0, v1, v2, v3, o_ref, s_sc):
    i = pl.program_id(1)
    kk = jnp.concatenate([k0[...], k1[...], k2[...], k3[...]], 0)
    vt = jnp.concatenate([v0[...], v1[...], v2[...], v3[...]], 1)
    nk = kk.shape[0]
    kpos = i * TQ_A - WINDOW + lax.broadcasted_iota(jnp.int32, (nk, 1), 0)
    qpos = i * TQ_A + lax.broadcasted_iota(jnp.int32, (1, TQ_A), 1)
    valid = (kpos >= 0) & (kpos < seq) & (jnp.abs(kpos - qpos) <= WINDOW)
    bias = jnp.where(valid, 0.0, NEG_BIG)

    def scores(h):
        qt = q_ref[:, h * LANES:(h + 1) * LANES].astype(F32).T.astype(BF16)
        s = jnp.dot(kk, qt, preferred_element_type=F32) + bias
        return s, jnp.max(s, 0, keepdims=True)

    s0, m_tile = scores(0)
    s_sc[0] = s0
    for h in range(A_HEADS):
        g = h // A_GROUP
        if h + 1 < A_HEADS:
            s_next, mt_next = scores(h + 1)
            s_sc[(h + 1) % 2] = s_next
        sink = sink_ref[0, h] * LOG2_E
        m = jnp.maximum(m_tile, sink)
        p = jnp.exp2(s_sc[h % 2] - m).astype(BF16)
        acc = jnp.dot(vt[g * VT_ROWS:(g + 1) * VT_ROWS, :], p, preferred_element_type=F32)
        den = acc[A_HEAD_DIM:A_HEAD_DIM + 1, :] + jnp.exp2(sink - m)
        o_ref[h * A_HEAD_DIM:(h + 1) * A_HEAD_DIM, :] = acc[:A_HEAD_DIM, :] / den
        if h + 1 < A_HEADS:
            m_tile = mt_next


def _attention_a(qa, ka, vat, sink, batch, seq):
    t = qa.shape[0]
    nq = seq // TQ_A
    nb = seq // WINDOW
    per = TQ_A // WINDOW
    assert per + 2 == 4

    def halo(j):
        return lambda b, i: b * nb + jnp.clip(per * i - 1 + j, 0, nb - 1)

    k_specs = [pl.BlockSpec((WINDOW, A_KV_COLS), lambda b, i, f=halo(j): (f(b, i), 0)) for j in range(per + 2)]
    v_specs = [pl.BlockSpec((A_KV_HEADS * VT_ROWS, WINDOW), lambda b, i, f=halo(j): (0, f(b, i)))
               for j in range(per + 2)]
    return pl.pallas_call(
        functools.partial(_attn_a_body, seq),
        grid=(batch, nq),
        in_specs=[pl.BlockSpec(memory_space=pltpu.SMEM),
                  pl.BlockSpec((TQ_A, A_HEADS * LANES), lambda b, i: (b * nq + i, 0))] + k_specs + v_specs,
        out_specs=pl.BlockSpec((A_Q_COLS, TQ_A), lambda b, i: (0, b * nq + i)),
        out_shape=jax.ShapeDtypeStruct((A_Q_COLS, t), F32),
        scratch_shapes=[pltpu.VMEM((2, TQ_A + 2 * WINDOW, TQ_A), F32)],
        compiler_params=_cparams(("parallel", "parallel")),
        name="window_attn",
    )(sink, qa, *([ka] * (per + 2)), *([vat] * (per + 2)))


def _attn_b_body(nkt, q_ref, k_ref, vt_ref, o_ref, s_sc, acc_sc):
    qt = q_ref[...].astype(F32).T.astype(BF16)
    acc_sc[...] = jnp.zeros(acc_sc.shape, F32)

    def scores(j):
        start = pl.multiple_of(j * TK_B, TK_B)
        s = jnp.dot(k_ref[pl.ds(start, TK_B), :], qt, preferred_element_type=F32)
        return s, jnp.max(s, 0, keepdims=True)

    def update(j, s, m, m_tile):
        start = pl.multiple_of(j * TK_B, TK_B)
        m_new = jnp.maximum(m, m_tile)
        alpha = jnp.exp2(m - m_new)
        p = jnp.exp2(s - m_new).astype(BF16)
        pv = jnp.dot(vt_ref[:, pl.ds(start, TK_B)], p, preferred_element_type=F32)
        acc_sc[...] = alpha * acc_sc[...] + pv
        return m_new

    s0, mt0 = scores(0)
    s_sc[0] = s0

    per_iter = min(KT_PER_ITER, nkt)
    assert nkt % per_iter == 0 and per_iter % 2 == 0

    def group(jj, carry):
        m, m_tile = carry
        for u in range(per_iter):
            j = per_iter * jj + u
            s_next, mt_next = scores(jnp.minimum(j + 1, nkt - 1))
            s_sc[(u + 1) % 2] = s_next
            m = update(j, s_sc[u % 2], m, m_tile)
            m_tile = mt_next
        return m, m_tile

    m0 = jnp.full((1, TQ_B), -jnp.inf, F32)
    lax.fori_loop(0, nkt // per_iter, group, (m0, mt0))
    o_ref[...] = acc_sc[:B_V, :] / acc_sc[B_V:B_V + 1, :]


def _attention_b(qb, kb, vt, batch, seq):
    t = qb.shape[0]
    nq = seq // TQ_B
    nkt = seq // TK_B
    return pl.pallas_call(
        functools.partial(_attn_b_body, nkt),
        grid=(batch * B_HEADS, nq),
        in_specs=[pl.BlockSpec((TQ_B, LANES), lambda bh, i: ((bh // B_HEADS) * nq + i, bh % B_HEADS)),
                  pl.BlockSpec((seq, LANES), lambda bh, i: (bh // B_HEADS, bh % B_HEADS)),
                  pl.BlockSpec((VT_ROWS, seq), lambda bh, i: (bh % B_HEADS, bh // B_HEADS))],
        out_specs=pl.BlockSpec((B_V, TQ_B), lambda bh, i: (bh % B_HEADS, (bh // B_HEADS) * nq + i)),
        out_shape=jax.ShapeDtypeStruct((B_HEADS * B_V, t), F32),
        scratch_shapes=[pltpu.VMEM((2, TK_B, TQ_B), F32), pltpu.VMEM((VT_ROWS, TQ_B), F32)],
        compiler_params=_cparams(("parallel", "arbitrary")),
        name="mla_attn",
    )(qb, kb, vt)


R_E1, R_E2, R_RANK1, R_RANK2, R_W1, R_W2 = range(6)
GROUP_LANE0 = N_EXPERTS


def _lane_pick(slab, lane, idx):
    return jnp.sum(jnp.where(lane == idx, slab, 0.0), -1, keepdims=True)


def _outproj_body(oa_ref, ob_ref, x_ref, ga_ref, gb_ref, woa_ref, wob_ref, lg0_ref, lb0_ref, lg1_ref, lb1_ref,
                  wrh_ref, wrl_ref, br_ref, h1_ref, h1p_ref, route_ref, counts_ref, carry_sc):
    @pl.when(pl.program_id(0) == 0)
    def _():
        carry_sc[...] = jnp.zeros(carry_sc.shape, F32)

    oa = oa_ref[...].T
    ob = ob_ref[...].T
    na = oa * lax.rsqrt(jnp.mean(oa * oa, -1, keepdims=True) + RMS_EPS) * ga_ref[...]
    nb = ob * lax.rsqrt(jnp.mean(ob * ob, -1, keepdims=True) + RMS_EPS) * gb_ref[...]
    mixed = (jnp.dot(na.astype(BF16), woa_ref[...], preferred_element_type=F32)
             + jnp.dot(nb.astype(BF16), wob_ref[...], preferred_element_type=F32))
    h0 = _layer_norm(x_ref[...], lg0_ref[...], lb0_ref[...])
    h1 = _layer_norm(ALPHA * h0 + mixed, lg1_ref[...], lb1_ref[...])
    h1_ref[...] = h1

    hi = h1.astype(BF16)
    hi32 = lax.bitcast_convert_type(hi.astype(F32), jnp.uint32)
    half = D_MODEL // 2
    h1p_ref[...] = hi32[:, :half] | (hi32[:, half:] >> 16)

    lo = (h1 - hi.astype(F32)).astype(BF16)
    logits = (jnp.dot(hi, wrh_ref[...], preferred_element_type=F32)
              + jnp.dot(lo, wrh_ref[...], preferred_element_type=F32)
              + jnp.dot(hi, wrl_ref[...], preferred_element_type=F32)) + br_ref[...]

    tm = logits.shape[0]
    lane = lax.broadcasted_iota(jnp.int32, (1, LANES), 1).astype(F32)
    far = float(LANES)
    is_g = (lane >= GROUP_LANE0) & (lane < GROUP_LANE0 + N_GROUPS)
    gl = jnp.where(is_g, logits, -jnp.inf)
    gmax = jnp.max(gl, -1, keepdims=True)
    gidx = jnp.min(jnp.where(gl == gmax, lane, far), -1, keepdims=True) - GROUP_LANE0
    pg = 1.0 / jnp.sum(jnp.where(is_g, jnp.exp(jnp.where(is_g, logits, gmax) - gmax), 0.0), -1, keepdims=True)
    in_grp = (lane >= gidx * EXPERTS_PER_GROUP) & (lane < (gidx + 1.0) * EXPERTS_PER_GROUP)
    el = jnp.where(in_grp, logits, -jnp.inf)
    v1 = jnp.max(el, -1, keepdims=True)
    i1 = jnp.min(jnp.where(el == v1, lane, far), -1, keepdims=True)
    el2 = jnp.where(lane == i1, -jnp.inf, el)
    v2 = jnp.max(el2, -1, keepdims=True)
    i2 = jnp.min(jnp.where(el2 == v2, lane, far), -1, keepdims=True)
    ex = jnp.exp(v2 - v1)
    w1 = pg / (1.0 + ex)
    w2 = pg * ex / (1.0 + ex)

    onehot = jnp.where((lane == i1) | (lane == i2), 1.0, 0.0)
    r_i = lax.broadcasted_iota(jnp.int32, (tm, tm), 0)
    c_i = lax.broadcasted_iota(jnp.int32, (tm, tm), 1)
    lower = jnp.where(c_i < r_i, 1.0, 0.0).astype(BF16)
    prefix = jnp.dot(lower, onehot.astype(BF16), preferred_element_type=F32) + carry_sc[0:1, :]
    rank1 = _lane_pick(prefix, lane, i1)
    rank2 = _lane_pick(prefix, lane, i2)
    new_carry = carry_sc[0:1, :] + jnp.sum(onehot, 0, keepdims=True)
    carry_sc[...] = jnp.broadcast_to(new_carry, carry_sc.shape)
    counts_ref[...] = jnp.broadcast_to(new_carry, counts_ref.shape)

    slab = jnp.zeros((tm, LANES), F32)
    for idx, val in ((R_E1, i1), (R_E2, i2), (R_RANK1, rank1), (R_RANK2, rank2), (R_W1, w1), (R_W2, w2)):
        slab = jnp.where(lane == idx, val, slab)
    route_ref[...] = slab


def _outproj_router(oa, ob, x2, consts):
    t = x2.shape[0]
    tm = TM_PROJ
    full = lambda a: pl.BlockSpec(a.shape, lambda i: (0,) * a.ndim)
    tile = lambda n: pl.BlockSpec((tm, n), lambda i: (i, 0))
    return pl.pallas_call(
        _outproj_body,
        grid=(t // tm,),
        in_specs=[pl.BlockSpec((A_Q_COLS, tm), lambda i: (0, i)),
                  pl.BlockSpec((B_HEADS * B_V, tm), lambda i: (0, i)), tile(D_MODEL)]
        + [full(a) for a in consts],
        out_specs=[tile(D_MODEL), tile(D_MODEL // 2), tile(LANES), pl.BlockSpec((8, LANES), lambda i: (0, 0))],
        out_shape=[jax.ShapeDtypeStruct((t, D_MODEL), F32), jax.ShapeDtypeStruct((t, D_MODEL // 2), jnp.uint32),
                   jax.ShapeDtypeStruct((t, LANES), F32), jax.ShapeDtypeStruct((8, LANES), F32)],
        scratch_shapes=[pltpu.VMEM((8, LANES), F32)],
        compiler_params=_cparams(("arbitrary",)),
        name="out_proj_router",
    )(oa, ob, x2, *consts)


def _slot_body(n_tiles, route_ref, counts_ref, slots_ref, tmap_ref):
    lane_i = lax.broadcasted_iota(jnp.int32, (1, LANES), 1)
    lane = lane_i.astype(F32)
    cnt = jnp.where(lane_i < N_EXPERTS, counts_ref[0:1, :], 0.0)
    padded = jnp.floor((cnt + (TE - 1)) * (1.0 / TE)) * TE
    incl = padded
    for k in (1, 2, 4, 8, 16):
        incl = incl + jnp.where(lane_i >= k, pltpu.roll(incl, k, 1), 0.0)
    off = incl - padded

    route = route_ref[...]
    e1 = _lane_pick(route, lane, R_E1)
    e2 = _lane_pick(route, lane, R_E2)
    s1 = _lane_pick(jnp.broadcast_to(off, route.shape), lane, e1) + _lane_pick(route, lane, R_RANK1)
    s2 = _lane_pick(jnp.broadcast_to(off, route.shape), lane, e2) + _lane_pick(route, lane, R_RANK2)
    slab = jnp.where(lane == 0, s1, jnp.where(lane == 1, s2, 0.0))
    slots_ref[...] = slab.astype(jnp.int32)

    row0 = lax.broadcasted_iota(jnp.int32, (n_tiles, 1), 0).astype(F32) * TE
    is_e = lane_i < N_EXPERTS
    te = jnp.sum(jnp.where(is_e & (incl <= row0), 1.0, 0.0), -1, keepdims=True)
    te = jnp.minimum(te, N_EXPERTS - 1.0)
    used = jnp.sum(jnp.where(lane == te, cnt - (row0 - off), 0.0), -1, keepdims=True)
    used = jnp.clip(used, 0.0, float(TE))
    tmap = jnp.where(lane == 0, te, jnp.where(lane == 1, used, 0.0))
    tmap_ref[...] = tmap.astype(jnp.int32)


def _slot_assign(route, counts, n_tiles):
    t = route.shape[0]
    tm = min(TM_SLOT, t)
    return pl.pallas_call(
        functools.partial(_slot_body, n_tiles),
        grid=(t // tm,),
        in_specs=[pl.BlockSpec((tm, LANES), lambda i: (i, 0)), pl.BlockSpec((8, LANES), lambda i: (0, 0))],
        out_specs=[pl.BlockSpec((tm, LANES), lambda i: (i, 0)), pl.BlockSpec((n_tiles, LANES), lambda i: (0, 0))],
        out_shape=[jax.ShapeDtypeStruct((t, LANES), jnp.int32), jax.ShapeDtypeStruct((n_tiles, LANES), jnp.int32)],
        compiler_params=_cparams(("arbitrary",)),
        name="slot_assign",
    )(route, counts)


def _dispatch_body(tm, slots_ref, src_ref, zeros_ref, dst_ref, sem):
    del zeros_ref

    def row_copy(j, k):
        return pltpu.make_async_copy(src_ref.at[pl.ds(j, 1), :],
                                     dst_ref.at[pl.ds(slots_ref[0, k, j], 1), :], sem)

    def start(j, carry):
        row_copy(j, 0).start(priority=0)
        row_copy(j, 1).start(priority=1)
        return carry

    def wait(j, carry):
        row_copy(j, 0).wait()
        row_copy(j, 1).wait()
        return carry

    lax.fori_loop(0, tm, start, 0, unroll=8)
    lax.fori_loop(0, tm, wait, 0, unroll=8)


def _dispatch(slots3, h1p, n_rows):
    t = h1p.shape[0]
    tm = slots3.shape[2]
    return pl.pallas_call(
        functools.partial(_dispatch_body, tm),
        grid=(t // tm,),
        in_specs=[pl.BlockSpec((1, 2, tm), lambda i: (i, 0, 0), memory_space=pltpu.SMEM),
                  pl.BlockSpec((tm, h1p.shape[1]), lambda i: (i, 0)), pl.BlockSpec(memory_space=pl.ANY)],
        out_specs=pl.BlockSpec(memory_space=pl.ANY),
        out_shape=jax.ShapeDtypeStruct((n_rows, h1p.shape[1]), h1p.dtype),
        scratch_shapes=[pltpu.SemaphoreType.DMA(())],
        input_output_aliases={2: 0},
        compiler_params=_cparams(("arbitrary",)),
        name="moe_dispatch",
    )(slots3, h1p, jnp.zeros((n_rows, h1p.shape[1]), h1p.dtype))


def _expert_body(te_ref, used_ref, xs_ref, wg_ref, wu_ref, wd_ref, y_ref, wg_sc, wu_sc, wd_sc):
    r = pl.program_id(0)
    used = used_ref[r]
    prev = te_ref[jnp.maximum(r - 1, 0)]

    @pl.when((r == 0) | (te_ref[r] != prev))
    def _():
        wg_sc[...] = wg_ref[0].astype(BF16)
        wu_sc[...] = wu_ref[0].astype(BF16)
        wd_sc[...] = wd_ref[0].astype(BF16)

    @pl.when(used > 0)
    def _():
        words = xs_ref[...]
        xa = lax.bitcast_convert_type(words & jnp.uint32(0xFFFF0000), F32).astype(BF16)
        xb = lax.bitcast_convert_type(words << 16, F32).astype(BF16)
        half = D_MODEL // 2
        g = (jnp.dot(xa, wg_sc[:half, :], preferred_element_type=F32)
             + jnp.dot(xb, wg_sc[half:, :], preferred_element_type=F32))
        u = (jnp.dot(xa, wu_sc[:half, :], preferred_element_type=F32)
             + jnp.dot(xb, wu_sc[half:, :], preferred_element_type=F32))
        hid = (g * (1.0 / (1.0 + jnp.exp(-g))) * u).astype(BF16)
        y_ref[...] = jnp.dot(hid, wd_sc[...], preferred_element_type=F32)

    @pl.when(used <= 0)
    def _():
        y_ref[...] = jnp.zeros(y_ref.shape, F32)


def _expert_mlp(tile_expert, tile_used, xs, w_gate, w_up, w_down):
    n_rows = xs.shape[0]
    n_tiles = n_rows // TE
    grid_spec = pltpu.PrefetchScalarGridSpec(
        num_scalar_prefetch=2,
        grid=(n_tiles,),
        in_specs=[pl.BlockSpec((TE, D_MODEL // 2), lambda r, te, us: (r, 0)),
                  pl.BlockSpec((1, D_MODEL, EXPERT_FF), lambda r, te, us: (te[r], 0, 0)),
                  pl.BlockSpec((1, D_MODEL, EXPERT_FF), lambda r, te, us: (te[r], 0, 0)),
                  pl.BlockSpec((1, EXPERT_FF, D_MODEL), lambda r, te, us: (te[r], 0, 0))],
        out_specs=pl.BlockSpec((TE, D_MODEL), lambda r, te, us: (r, 0)),
        scratch_shapes=[pltpu.VMEM((D_MODEL, EXPERT_FF), BF16), pltpu.VMEM((D_MODEL, EXPERT_FF), BF16),
                        pltpu.VMEM((EXPERT_FF, D_MODEL), BF16)],
    )
    return pl.pallas_call(
        _expert_body,
        grid_spec=grid_spec,
        out_shape=jax.ShapeDtypeStruct((n_rows, D_MODEL), F32),
        compiler_params=_cparams(("arbitrary",)),
        name="expert_mlp",
    )(tile_expert, tile_used, xs, w_gate, w_up, w_down)


def _combine_body(tm, slots_ref, h1_ref, route_ref, g_ref, b_ref, y_ref, o_ref, ybuf, sem):
    def row_copy(j, k):
        return pltpu.make_async_copy(y_ref.at[pl.ds(slots_ref[0, k, j], 1), :],
                                     ybuf.at[k, pl.ds(j, 1), :], sem)

    def start(j, carry):
        row_copy(j, 0).start(priority=0)
        row_copy(j, 1).start(priority=1)
        return carry

    def wait(j, carry):
        row_copy(j, 0).wait()
        row_copy(j, 1).wait()
        return carry

    lax.fori_loop(0, tm, start, 0, unroll=8)
    lax.fori_loop(0, tm, wait, 0, unroll=8)

    lane = lax.broadcasted_iota(jnp.int32, (1, LANES), 1).astype(F32)
    route = route_ref[...]
    w1 = _lane_pick(route, lane, R_W1)
    w2 = _lane_pick(route, lane, R_W2)
    ffn = w1 * ybuf[0] + w2 * ybuf[1]
    o_ref[...] = _layer_norm(ALPHA * h1_ref[...] + ffn, g_ref[...], b_ref[...])


def _combine(slots3, h1, route, ln_g, ln_b, y):
    t = h1.shape[0]
    tm = slots3.shape[2]
    return pl.pallas_call(
        functools.partial(_combine_body, tm),
        grid=(t // tm,),
        in_specs=[pl.BlockSpec((1, 2, tm), lambda i: (i, 0, 0), memory_space=pltpu.SMEM),
                  pl.BlockSpec((tm, D_MODEL), lambda i: (i, 0)),
                  pl.BlockSpec((tm, LANES), lambda i: (i, 0)),
                  pl.BlockSpec((1, D_MODEL), lambda i: (0, 0)),
                  pl.BlockSpec((1, D_MODEL), lambda i: (0, 0)),
                  pl.BlockSpec(memory_space=pl.ANY)],
        out_specs=pl.BlockSpec((tm, D_MODEL), lambda i: (i, 0)),
        out_shape=jax.ShapeDtypeStruct((t, D_MODEL), F32),
        scratch_shapes=[pltpu.VMEM((2, tm, D_MODEL), F32), pltpu.SemaphoreType.DMA(())],
        compiler_params=_cparams(("arbitrary",)),
        name="moe_combine",
    )(slots3, h1, route, ln_g, ln_b, y)


def _slot_blocks(slots, tm):
    t = slots.shape[0]
    return slots[:, :2].reshape(t // tm, tm, 2).transpose(0, 2, 1)


def kernel(x, positions, ln_emb_g, ln_emb_b, w_in, a_sink, q_a_norm_g, w_q_b, kv_a_norm_g, w_kv_b, out_norm_a_g, out_norm_b_g, w_out, ln_attn_g, ln_attn_b, w_group, b_group, w_expert, b_expert, w_gate, w_up, w_down, ln_ffn_g, ln_ffn_b):
    batch, seq, d = x.shape
    t = batch * seq
    x2 = x.reshape(t, d)
    row = lambda a: a.reshape(1, -1).astype(F32)

    tables = _rope_tables(positions)
    w1, wva, wq, wk, wv = _proj_weights(w_in[0], w_q_b[0], w_kv_b[0])
    qa, ka, vat, qb, kb, vt = _projections(x2, row(ln_emb_g), row(ln_emb_b), w1, wva, wq, wk, wv,
                                           row(q_a_norm_g[0]), row(kv_a_norm_g[0]), tables)
    oa = _attention_a(qa, ka, vat, a_sink[0].reshape(1, A_HEADS).astype(F32), batch, seq)
    ob = _attention_b(qb, kb, vt, batch, seq)

    wo = w_out[0]
    woa = wo[:A_Q_COLS].astype(BF16)
    wob = wo[A_Q_COLS:].astype(BF16)
    gb = row(out_norm_b_g[0])
    wr = jnp.concatenate([w_expert[0], w_group[0],
                          jnp.zeros((d, LANES - N_EXPERTS - N_GROUPS), F32)], -1)
    wr_hi = wr.astype(BF16)
    wr_lo = (wr - wr_hi.astype(F32)).astype(BF16)
    br = jnp.concatenate([b_expert[0], b_group[0], jnp.zeros((LANES - N_EXPERTS - N_GROUPS,), F32)]).reshape(1, LANES)
    consts = [row(out_norm_a_g[0]), gb, woa, wob, row(ln_emb_g), row(ln_emb_b), row(ln_attn_g[0]), row(ln_attn_b[0]),
              wr_hi, wr_lo, br]
    h1, h1p, route, counts = _outproj_router(oa, ob, x2, consts)

    n_tiles = (2 * t) // TE + N_EXPERTS
    slots, tmap = _slot_assign(route, counts, n_tiles)
    xs = _dispatch(_slot_blocks(slots, min(TM_DISP, t)), h1p, n_tiles * TE)
    y = _expert_mlp(tmap[:, 0], tmap[:, 1], xs, w_gate[0], w_up[0], w_down[0])
    out = _combine(_slot_blocks(slots, TM_COMB), h1, route, row(ln_ffn_g[0]), row(ln_ffn_b[0]), y)
    return out.reshape(batch, seq, d)
```

```python
import functools

import numpy as np
import jax
import jax.numpy as jnp
from jax import lax
from jax.experimental import pallas as pl
from jax.experimental.pallas import tpu as pltpu

F32 = jnp.float32
BF16 = jnp.bfloat16

D_MODEL = 1024
A_HEADS = 8
A_KV_HEADS = 2
A_HEAD_DIM = 64
WINDOW = 128
B_HEADS = 8
B_NOPE = 64
B_ROPE = 32
B_V = 64
B_Q_RANK = 768
B_KV_RANK = 256
ROPE_THETA = 10000.0
N_GROUPS = 4
EXPERTS_PER_GROUP = 8
N_EXPERTS = N_GROUPS * EXPERTS_PER_GROUP
EXPERT_FF = 256
LN_EPS = 1e-5
RMS_EPS = 1e-6
DEPTH = 1
ALPHA = (2 * DEPTH) ** 0.25
NEG_BIG = -1e30
LOG2_E = 1.4426950408889634

LANES = 128
A_Q_COLS = A_HEADS * A_HEAD_DIM
A_KV_COLS = A_KV_HEADS * A_HEAD_DIM
PAD_COLS = B_HEADS * LANES
BF16_SUBLANES = 16
VT_ROWS = B_V + BF16_SUBLANES
A_GROUP = A_HEADS // A_KV_HEADS

C_QA = 0
C_KA = C_QA + A_HEADS * LANES
C_CQ = C_KA + A_KV_COLS
C_CKV = C_CQ + B_Q_RANK
C_KR = C_CKV + B_KV_RANK
W1_COLS = C_KR + LANES

TM_PROJ = 256
TQ_A = 256
TQ_B = 512
TK_B = 512
KT_PER_ITER = 16
TE = 384
TM_SLOT = 1024
TM_DISP = 512
VMEM_LIMIT = 56 * 1024 * 1024


def _cparams(sem):
    return pltpu.CompilerParams(dimension_semantics=sem, vmem_limit_bytes=VMEM_LIMIT)


def _layer_norm(x, g, b):
    mu = jnp.mean(x, -1, keepdims=True)
    xc = x - mu
    var = jnp.mean(xc * xc, -1, keepdims=True)
    return xc * lax.rsqrt(var + LN_EPS) * g + b


def _rope_block(xc, cos, sin_signed, first_half, half):
    rot = jnp.where(first_half, pltpu.roll(xc, LANES - half, 1), pltpu.roll(xc, half, 1))
    return xc * cos + rot * sin_signed


HALF_A = A_HEAD_DIM // 2
HALF_B = B_ROPE // 2


def _rope_table_body(pos_ref, inv_ref, cosa_ref, sina_ref, cosb_ref, sinb_ref):
    ang = pos_ref[...].astype(F32) * inv_ref[...]
    cc = jnp.cos(ang)
    ss = jnp.sin(ang)
    lane = lax.broadcasted_iota(jnp.int32, (1, LANES), 1)

    def table_a(x):
        out = x
        for k in range(1, LANES // HALF_A):
            out = jnp.where(lane >= k * HALF_A, pltpu.roll(x, k * HALF_A, 1), out)
        return out

    def table_b(x, fill):
        lo = pltpu.roll(x, B_NOPE - HALF_A, 1)
        hi = pltpu.roll(x, B_NOPE - HALF_A + HALF_B, 1)
        out = jnp.where(lane < B_NOPE + HALF_B, lo, hi)
        return jnp.where((lane >= B_NOPE) & (lane < B_NOPE + B_ROPE), out, fill)

    sign_a = jnp.where(lane % A_HEAD_DIM < HALF_A, -1.0, 1.0)
    sign_b = jnp.where(lane < B_NOPE + HALF_B, -1.0, 1.0)
    cosa_ref[...] = table_a(cc)
    sina_ref[...] = table_a(ss) * sign_a
    cosb_ref[...] = table_b(cc, 1.0)
    sinb_ref[...] = table_b(ss, 0.0) * sign_b


def _rope_tables(positions):
    t = positions.shape[0] * positions.shape[1]
    tm = 512
    inv_a = 1.0 / (ROPE_THETA ** (np.arange(0, A_HEAD_DIM, 2, dtype=np.float64) / A_HEAD_DIM))
    inv_b = 1.0 / (ROPE_THETA ** (np.arange(0, B_ROPE, 2, dtype=np.float64) / B_ROPE))
    inv = np.zeros((1, LANES))
    inv[0, :HALF_A] = inv_a
    inv[0, HALF_A:HALF_A + HALF_B] = inv_b
    tab = pl.BlockSpec((tm, LANES), lambda i: (i, 0))
    return pl.pallas_call(
        _rope_table_body,
        grid=(t // tm,),
        in_specs=[pl.BlockSpec((tm, 1), lambda i: (i, 0)), pl.BlockSpec((1, LANES), lambda i: (0, 0))],
        out_specs=[tab, tab, tab, tab],
        out_shape=[jax.ShapeDtypeStruct((t, LANES), F32)] * 4,
        compiler_params=_cparams(("parallel",)),
        name="rope_tables",
    )(positions.reshape(t, 1), jnp.asarray(inv, F32))


def _ones_rows(vt):
    vrow = lax.broadcasted_iota(jnp.int32, (vt.shape[0], 1), 0)
    return jnp.where(vrow % VT_ROWS >= B_V, 1.0, vt)


def _proj_body(x_ref, g_ref, b_ref, w1_ref, wva_ref, wq_ref, wk_ref, wv_ref, gq_ref, gkv_ref,
               cosa_ref, sina_ref, cosb_ref, sinb_ref,
               qa_ref, ka_ref, vat_ref, qb_ref, kb_ref, vt_ref):
    h = _layer_norm(x_ref[...], g_ref[...], b_ref[...]).astype(BF16)
    lane = lax.broadcasted_iota(jnp.int32, (1, LANES), 1)
    first_a = (lane % A_HEAD_DIM) < (A_HEAD_DIM // 2)
    first_b = (lane % B_ROPE) < (B_ROPE // 2)
    cosa, sina = cosa_ref[...], sina_ref[...]
    cosb, sinb = cosb_ref[...], sinb_ref[...]

    def proj(c0, width):
        return jnp.dot(h, w1_ref[:, c0:c0 + width], preferred_element_type=F32)

    qa = proj(C_QA, A_HEADS * LANES)
    scale_a = A_HEAD_DIM ** -0.5 * LOG2_E
    for c in range(A_HEADS):
        blk = _rope_block(qa[:, c * LANES:(c + 1) * LANES], cosa, sina, first_a, HALF_A)
        qa_ref[:, c * LANES:(c + 1) * LANES] = (blk * scale_a).astype(BF16)
    ka_ref[...] = _rope_block(proj(C_KA, A_KV_COLS), cosa, sina, first_a, HALF_A).astype(BF16)
    vat = lax.dot_general(wva_ref[...], h, (((1,), (1,)), ((), ())), preferred_element_type=F32)
    vat_ref[...] = _ones_rows(vat).astype(BF16)

    cq = proj(C_CQ, B_Q_RANK)
    cq = cq * lax.rsqrt(jnp.mean(cq * cq, -1, keepdims=True) + RMS_EPS) * gq_ref[...]
    qb = jnp.dot(cq.astype(BF16), wq_ref[...], preferred_element_type=F32)
    scale_b = (B_NOPE + B_ROPE) ** -0.5 * LOG2_E
    for c in range(B_HEADS):
        blk = _rope_block(qb[:, c * LANES:(c + 1) * LANES], cosb, sinb, first_b, HALF_B)
        qb_ref[:, c * LANES:(c + 1) * LANES] = (blk * scale_b).astype(BF16)

    ckv = proj(C_CKV, B_KV_RANK)
    ckv = (ckv * lax.rsqrt(jnp.mean(ckv * ckv, -1, keepdims=True) + RMS_EPS) * gkv_ref[...]).astype(BF16)
    kr = _rope_block(proj(C_KR, LANES), cosb, sinb, first_b, HALF_B)
    kn = jnp.dot(ckv, wk_ref[...], preferred_element_type=F32)
    for c in range(B_HEADS):
        kb_ref[:, c * LANES:(c + 1) * LANES] = (kn[:, c * LANES:(c + 1) * LANES] + kr).astype(BF16)
    vt = lax.dot_general(wv_ref[...], ckv, (((1,), (1,)), ((), ())), preferred_element_type=F32)
    vt_ref[...] = _ones_rows(vt).astype(BF16)


def _proj_weights(w_in, w_q_b, w_kv_b):
    w = w_in
    d = w.shape[0]
    o = A_Q_COLS
    ka = w[:, o:o + A_KV_COLS]; o += A_KV_COLS
    va = w[:, o:o + A_KV_COLS]; o += A_KV_COLS
    cq = w[:, o:o + B_Q_RANK]; o += B_Q_RANK
    ckv = w[:, o:o + B_KV_RANK]; o += B_KV_RANK
    kr = w[:, o:o + B_ROPE]
    kr_blk = jnp.concatenate([jnp.zeros((d, B_NOPE), w.dtype), kr,
                              jnp.zeros((d, LANES - B_NOPE - B_ROPE), w.dtype)], -1)
    qa = w[:, :A_Q_COLS].reshape(d, A_KV_HEADS, A_GROUP, A_HEAD_DIM)
    qa = jnp.stack([jnp.pad(qa[:, g], ((0, 0), (0, 0), (g * A_HEAD_DIM, LANES - (g + 1) * A_HEAD_DIM)))
                    for g in range(A_KV_HEADS)], 1).reshape(d, A_HEADS * LANES)
    w1 = jnp.concatenate([qa, ka, cq, ckv, kr_blk], -1).astype(BF16)
    wva = jnp.pad(va.reshape(d, A_KV_HEADS, A_HEAD_DIM), ((0, 0), (0, 0), (0, VT_ROWS - A_HEAD_DIM)))
    wva = wva.reshape(d, A_KV_HEADS * VT_ROWS).T.astype(BF16)
    wq = w_q_b.reshape(B_Q_RANK, B_HEADS, B_NOPE + B_ROPE)
    wq = jnp.pad(wq, ((0, 0), (0, 0), (0, LANES - B_NOPE - B_ROPE))).reshape(B_Q_RANK, PAD_COLS).astype(BF16)
    wkv = w_kv_b.reshape(B_KV_RANK, B_HEADS, B_NOPE + B_V)
    wk = jnp.pad(wkv[:, :, :B_NOPE], ((0, 0), (0, 0), (0, LANES - B_NOPE))).reshape(B_KV_RANK, PAD_COLS).astype(BF16)
    wv = jnp.pad(wkv[:, :, B_NOPE:], ((0, 0), (0, 0), (0, VT_ROWS - B_V)))
    wv = wv.reshape(B_KV_RANK, B_HEADS * VT_ROWS).T.astype(BF16)
    return w1, wva, wq, wk, wv


def _projections(x2, ln_g, ln_b, w1, wva, wq, wk, wv, gq, gkv, tables):
    t = x2.shape[0]
    tm = TM_PROJ
    full = lambda a: pl.BlockSpec(a.shape, lambda i: (0,) * a.ndim)
    tile = lambda n: pl.BlockSpec((tm, n), lambda i: (i, 0))
    consts = [ln_g, ln_b, w1, wva, wq, wk, wv, gq, gkv]
    tok = lambda n: (tile(n), jax.ShapeDtypeStruct((t, n), BF16))
    feat = lambda r: (pl.BlockSpec((r, tm), lambda i: (0, i)), jax.ShapeDtypeStruct((r, t), BF16))
    outs = [tok(A_HEADS * LANES), tok(A_KV_COLS), feat(A_KV_HEADS * VT_ROWS),
            tok(PAD_COLS), tok(PAD_COLS), feat(B_HEADS * VT_ROWS)]
    return pl.pallas_call(
        _proj_body,
        grid=(t // tm,),
        in_specs=[tile(D_MODEL)] + [full(a) for a in consts] + [tile(LANES)] * 4,
        out_specs=[o[0] for o in outs],
        out_shape=[o[1] for o in outs],
        compiler_params=_cparams(("parallel",)),
        name="ln_in_proj",
    )(x2, *consts, *tables)


def _attn_a_body(seq, sink_ref, q_ref, k0, k1, k2, k3, v0, v1, v2, v3, o_ref, s_sc):
    i = pl.program_id(1)
    kk = jnp.concatenate([k0[...], k1[...], k2[...], k3[...]], 0)
    vt = jnp.concatenate([v0[...], v1[...], v2[...], v3[...]], 1)
    nk = kk.shape[0]
    kpos = i * TQ_A - WINDOW + lax.broadcasted_iota(jnp.int32, (nk, 1), 0)
    qpos = i * TQ_A + lax.broadcasted_iota(jnp.int32, (1, TQ_A), 1)
    valid = (kpos >= 0) & (kpos < seq) & (jnp.abs(kpos - qpos) <= WINDOW)
    bias = jnp.where(valid, 0.0, NEG_BIG)

    def scores(h):
        qt = q_ref[:, h * LANES:(h + 1) * LANES].astype(F32).T.astype(BF16)
        s = jnp.dot(kk, qt, preferred_element_type=F32) + bias
        return s, jnp.max(s, 0, keepdims=True)

    s0, m_tile = scores(0)
    s_sc[0] = s0
    for h in range(A_HEADS):
        g = h // A_GROUP
        if h + 1 < A_HEADS:
            s_next, mt_next = scores(h + 1)
            s_sc[(h + 1) % 2] = s_next
        sink = sink_ref[0, h] * LOG2_E
        m = jnp.maximum(m_tile, sink)
        p = jnp.exp2(s_sc[h % 2] - m).astype(BF16)
        acc = jnp.dot(vt[g * VT_ROWS:(g + 1) * VT_ROWS, :], p, preferred_element_type=F32)
        den = acc[A_HEAD_DIM:A_HEAD_DIM + 1, :] + jnp.exp2(sink - m)
        o_ref[h * A_HEAD_DIM:(h + 1) * A_HEAD_DIM, :] = acc[:A_HEAD_DIM, :] / den
        if h + 1 < A_HEADS:
            m_tile = mt_next


def _attention_a(qa, ka, vat, sink, batch, seq):
    t = qa.shape[0]
    nq = seq // TQ_A
    nb = seq // WINDOW
    per = TQ_A // WINDOW
    assert per + 2 == 4

    def halo(j):
        return lambda b, i: b * nb + jnp.clip(per * i - 1 + j, 0, nb - 1)

    k_specs = [pl.BlockSpec((WINDOW, A_KV_COLS), lambda b, i, f=halo(j): (f(b, i), 0)) for j in range(per + 2)]
    v_specs = [pl.BlockSpec((A_KV_HEADS * VT_ROWS, WINDOW), lambda b, i, f=halo(j): (0, f(b, i)))
               for j in range(per + 2)]
    return pl.pallas_call(
        functools.partial(_attn_a_body, seq),
        grid=(batch, nq),
        in_specs=[pl.BlockSpec(memory_space=pltpu.SMEM),
                  pl.BlockSpec((TQ_A, A_HEADS * LANES), lambda b, i: (b * nq + i, 0))] + k_specs + v_specs,
        out_specs=pl.BlockSpec((A_Q_COLS, TQ_A), lambda b, i: (0, b * nq + i)),
        out_shape=jax.ShapeDtypeStruct((A_Q_COLS, t), F32),
        scratch_shapes=[pltpu.VMEM((2, TQ_A + 2 * WINDOW, TQ_A), F32)],
        compiler_params=_cparams(("parallel", "parallel")),
        name="window_attn",
    )(sink, qa, *([ka] * (per + 2)), *([vat] * (per + 2)))


def _attn_b_body(nkt, q_ref, k_ref, vt_ref, o_ref, s_sc, acc_sc):
    qt = q_ref[...].astype(F32).T.astype(BF16)
    acc_sc[...] = jnp.zeros(acc_sc.shape, F32)

    def scores(j):
        start = pl.multiple_of(j * TK_B, TK_B)
        s = jnp.dot(k_ref[pl.ds(start, TK_B), :], qt, preferred_element_type=F32)
        return s, jnp.max(s, 0, keepdims=True)

    def update(j, s, m, m_tile):
        start = pl.multiple_of(j * TK_B, TK_B)
        m_new = jnp.maximum(m, m_tile)
        alpha = jnp.exp2(m - m_new)
        p = jnp.exp2(s - m_new).astype(BF16)
        pv = jnp.dot(vt_ref[:, pl.ds(start, TK_B)], p, preferred_element_type=F32)
        acc_sc[...] = alpha * acc_sc[...] + pv
        return m_new

    s0, mt0 = scores(0)
    s_sc[0] = s0

    per_iter = min(KT_PER_ITER, nkt)
    assert nkt % per_iter == 0 and per_iter % 2 == 0

    def group(jj, carry):
        m, m_tile = carry
        for u in range(per_iter):
            j = per_iter * jj + u
            s_next, mt_next = scores(jnp.minimum(j + 1, nkt - 1))
            s_sc[(u + 1) % 2] = s_next
            m = update(j, s_sc[u % 2], m, m_tile)
            m_tile = mt_next
        return m, m_tile

    m0 = jnp.full((1, TQ_B), -jnp.inf, F32)
    lax.fori_loop(0, nkt // per_iter, group, (m0, mt0))
    o_ref[...] = acc_sc[:B_V, :] / acc_sc[B_V:B_V + 1, :]


def _attention_b(qb, kb, vt, batch, seq):
    t = qb.shape[0]
    nq = seq // TQ_B
    nkt = seq // TK_B
    return pl.pallas_call(
        functools.partial(_attn_b_body, nkt),
        grid=(batch * B_HEADS, nq),
        in_specs=[pl.BlockSpec((TQ_B, LANES), lambda bh, i: ((bh // B_HEADS) * nq + i, bh % B_HEADS)),
                  pl.BlockSpec((seq, LANES), lambda bh, i: (bh // B_HEADS, bh % B_HEADS)),
                  pl.BlockSpec((VT_ROWS, seq), lambda bh, i: (bh % B_HEADS, bh // B_HEADS))],
        out_specs=pl.BlockSpec((B_V, TQ_B), lambda bh, i: (bh % B_HEADS, (bh // B_HEADS) * nq + i)),
        out_shape=jax.ShapeDtypeStruct((B_HEADS * B_V, t), F32),
        scratch_shapes=[pltpu.VMEM((2, TK_B, TQ_B), F32), pltpu.VMEM((VT_ROWS, TQ_B), F32)],
        compiler_params=_cparams(("parallel", "arbitrary")),
        name="mla_attn",
    )(qb, kb, vt)


R_CLASS, R_RANK = range(2)
M_WLO, M_WHI = range(2)
GROUP_LANE0 = N_EXPERTS
PAIRS_PER_GROUP = EXPERTS_PER_GROUP * (EXPERTS_PER_GROUP - 1) // 2
N_CLASSES = N_GROUPS * PAIRS_PER_GROUP
ROW_WORDS = D_MODEL + LANES


def _lane_pick(slab, lane, idx):
    return jnp.sum(jnp.where(lane == idx, slab, 0.0), -1, keepdims=True)


def _outproj_body(oa_ref, ob_ref, x_ref, ga_ref, gb_ref, woa_ref, wob_ref, lg0_ref, lb0_ref, lg1_ref, lb1_ref,
                  wrh_ref, wrl_ref, br_ref, rows_ref, route_ref, counts_ref, carry_sc):
    @pl.when(pl.program_id(0) == 0)
    def _():
        carry_sc[...] = jnp.zeros(carry_sc.shape, F32)

    oa = oa_ref[...].T
    ob = ob_ref[...].T
    na = oa * lax.rsqrt(jnp.mean(oa * oa, -1, keepdims=True) + RMS_EPS) * ga_ref[...]
    nb = ob * lax.rsqrt(jnp.mean(ob * ob, -1, keepdims=True) + RMS_EPS) * gb_ref[...]
    mixed = (jnp.dot(na.astype(BF16), woa_ref[...], preferred_element_type=F32)
             + jnp.dot(nb.astype(BF16), wob_ref[...], preferred_element_type=F32))
    h0 = _layer_norm(x_ref[...], lg0_ref[...], lb0_ref[...])
    h1 = _layer_norm(ALPHA * h0 + mixed, lg1_ref[...], lb1_ref[...])
    rows_ref[:, :D_MODEL] = h1

    hi = h1.astype(BF16)
    lo = (h1 - hi.astype(F32)).astype(BF16)
    logits = (jnp.dot(hi, wrh_ref[...], preferred_element_type=F32)
              + jnp.dot(lo, wrh_ref[...], preferred_element_type=F32)
              + jnp.dot(hi, wrl_ref[...], preferred_element_type=F32)) + br_ref[...]

    tm = logits.shape[0]
    lane = lax.broadcasted_iota(jnp.int32, (1, LANES), 1).astype(F32)
    far = float(LANES)
    is_g = (lane >= GROUP_LANE0) & (lane < GROUP_LANE0 + N_GROUPS)
    gl = jnp.where(is_g, logits, -jnp.inf)
    gmax = jnp.max(gl, -1, keepdims=True)
    gidx = jnp.min(jnp.where(gl == gmax, lane, far), -1, keepdims=True) - GROUP_LANE0
    pg = 1.0 / jnp.sum(jnp.where(is_g, jnp.exp(jnp.where(is_g, logits, gmax) - gmax), 0.0), -1, keepdims=True)
    in_grp = (lane >= gidx * EXPERTS_PER_GROUP) & (lane < (gidx + 1.0) * EXPERTS_PER_GROUP)
    el = jnp.where(in_grp, logits, -jnp.inf)
    v1 = jnp.max(el, -1, keepdims=True)
    i1 = jnp.min(jnp.where(el == v1, lane, far), -1, keepdims=True)
    el2 = jnp.where(lane == i1, -jnp.inf, el)
    v2 = jnp.max(el2, -1, keepdims=True)
    i2 = jnp.min(jnp.where(el2 == v2, lane, far), -1, keepdims=True)
    ex = jnp.exp(v2 - v1)
    w1 = pg / (1.0 + ex)
    w2 = pg * ex / (1.0 + ex)

    first_lo = i1 < i2
    a = jnp.minimum(i1, i2) - gidx * EXPERTS_PER_GROUP
    b = jnp.maximum(i1, i2) - gidx * EXPERTS_PER_GROUP
    pair = a * (2.0 * EXPERTS_PER_GROUP - 1.0 - a) * 0.5 + (b - a - 1.0)
    cls = gidx * PAIRS_PER_GROUP + pair
    w_lo = jnp.where(first_lo, w1, w2)
    w_hi = jnp.where(first_lo, w2, w1)
    rows_ref[:, D_MODEL:] = jnp.where(lane == M_WLO, w_lo, jnp.where(lane == M_WHI, w_hi, 0.0))

    onehot = jnp.where(lane == cls, 1.0, 0.0)
    r_i = lax.broadcasted_iota(jnp.int32, (tm, tm), 0)
    c_i = lax.broadcasted_iota(jnp.int32, (tm, tm), 1)
    lower = jnp.where(c_i < r_i, 1.0, 0.0).astype(BF16)
    prefix = jnp.dot(lower, onehot.astype(BF16), preferred_element_type=F32) + carry_sc[0:1, :]
    rank = _lane_pick(prefix, lane, cls)
    new_carry = carry_sc[0:1, :] + jnp.sum(onehot, 0, keepdims=True)
    carry_sc[...] = jnp.broadcast_to(new_carry, carry_sc.shape)
    counts_ref[...] = jnp.broadcast_to(new_carry, counts_ref.shape)
    route_ref[...] = jnp.where(lane == R_CLASS, cls, jnp.where(lane == R_RANK, rank, 0.0))


def _outproj_router(oa, ob, x2, consts):
    t = x2.shape[0]
    tm = TM_PROJ
    full = lambda a: pl.BlockSpec(a.shape, lambda i: (0,) * a.ndim)
    tile = lambda n: pl.BlockSpec((tm, n), lambda i: (i, 0))
    return pl.pallas_call(
        _outproj_body,
        grid=(t // tm,),
        in_specs=[pl.BlockSpec((A_Q_COLS, tm), lambda i: (0, i)),
                  pl.BlockSpec((B_HEADS * B_V, tm), lambda i: (0, i)), tile(D_MODEL)]
        + [full(a) for a in consts],
        out_specs=[tile(ROW_WORDS), tile(LANES), pl.BlockSpec((8, LANES), lambda i: (0, 0))],
        out_shape=[jax.ShapeDtypeStruct((t, ROW_WORDS), F32),
                   jax.ShapeDtypeStruct((t, LANES), F32), jax.ShapeDtypeStruct((8, LANES), F32)],
        scratch_shapes=[pltpu.VMEM((8, LANES), F32)],
        compiler_params=_cparams(("arbitrary",)),
        name="out_proj_router",
    )(oa, ob, x2, *consts)


T_CLASS, T_USED, T_ELO, T_EHI = range(4)


def _slot_body(n_tiles, route_ref, counts_ref, elo_ref, ehi_ref, slots_ref, tmap_ref):
    lane_i = lax.broadcasted_iota(jnp.int32, (1, LANES), 1)
    lane = lane_i.astype(F32)
    is_c = lane_i < N_CLASSES
    cnt = jnp.where(is_c, counts_ref[0:1, :], 0.0)
    padded = jnp.floor((cnt + (TE - 1)) * (1.0 / TE)) * TE
    incl = padded
    k = 1
    while k < LANES:
        incl = incl + jnp.where(lane_i >= k, pltpu.roll(incl, k, 1), 0.0)
        k *= 2
    off = incl - padded

    route = route_ref[...]
    cls = _lane_pick(route, lane, R_CLASS)
    slot = _lane_pick(jnp.broadcast_to(off, route.shape), lane, cls) + _lane_pick(route, lane, R_RANK)
    slots_ref[...] = jnp.where(lane == 0, slot, 0.0).astype(jnp.int32)

    row0 = lax.broadcasted_iota(jnp.int32, (n_tiles, 1), 0).astype(F32) * TE
    tc = jnp.sum(jnp.where(is_c & (incl <= row0), 1.0, 0.0), -1, keepdims=True)
    tc = jnp.minimum(tc, N_CLASSES - 1.0)
    at_tc = lane == tc
    used = jnp.clip(jnp.sum(jnp.where(at_tc, cnt - (row0 - off), 0.0), -1, keepdims=True), 0.0, float(TE))
    elo = jnp.sum(jnp.where(at_tc, elo_ref[...], 0.0), -1, keepdims=True)
    ehi = jnp.sum(jnp.where(at_tc, ehi_ref[...], 0.0), -1, keepdims=True)
    tmap = jnp.zeros((n_tiles, LANES), F32)
    for idx, val in ((T_CLASS, tc), (T_USED, used), (T_ELO, elo), (T_EHI, ehi)):
        tmap = jnp.where(lane == idx, val, tmap)
    tmap_ref[...] = tmap.astype(jnp.int32)


def _class_experts():
    elo = np.zeros((1, LANES))
    ehi = np.zeros((1, LANES))
    c = 0
    for g in range(N_GROUPS):
        for a in range(EXPERTS_PER_GROUP):
            for b in range(a + 1, EXPERTS_PER_GROUP):
                elo[0, c] = g * EXPERTS_PER_GROUP + a
                ehi[0, c] = g * EXPERTS_PER_GROUP + b
                c += 1
    return jnp.asarray(elo, F32), jnp.asarray(ehi, F32)


def _slot_assign(route, counts, n_tiles):
    t = route.shape[0]
    tm = min(TM_SLOT, t)
    row = pl.BlockSpec((1, LANES), lambda i: (0, 0))
    return pl.pallas_call(
        functools.partial(_slot_body, n_tiles),
        grid=(t // tm,),
        in_specs=[pl.BlockSpec((tm, LANES), lambda i: (i, 0)), pl.BlockSpec((8, LANES), lambda i: (0, 0)), row, row],
        out_specs=[pl.BlockSpec((tm, LANES), lambda i: (i, 0)), pl.BlockSpec((n_tiles, LANES), lambda i: (0, 0))],
        out_shape=[jax.ShapeDtypeStruct((t, LANES), jnp.int32), jax.ShapeDtypeStruct((n_tiles, LANES), jnp.int32)],
        compiler_params=_cparams(("arbitrary",)),
        name="slot_assign",
    )(route, counts, *_class_experts())


def _dispatch_body(tm, slots_ref, src_ref, zeros_ref, dst_ref, sem):
    del zeros_ref

    def row_copy(j):
        return pltpu.make_async_copy(src_ref.at[pl.ds(j, 1), :],
                                     dst_ref.at[pl.ds(slots_ref[0, 0, j], 1), :], sem)

    def start(j, carry):
        row_copy(j).start()
        return carry

    def wait(j, carry):
        row_copy(j).wait()
        return carry

    lax.fori_loop(0, tm, start, 0, unroll=8)
    lax.fori_loop(0, tm, wait, 0, unroll=8)


def _dispatch(slots3, rows, n_rows):
    t = rows.shape[0]
    tm = slots3.shape[2]
    return pl.pallas_call(
        functools.partial(_dispatch_body, tm),
        grid=(t // tm,),
        in_specs=[pl.BlockSpec((1, 1, tm), lambda i: (i, 0, 0), memory_space=pltpu.SMEM),
                  pl.BlockSpec((tm, ROW_WORDS), lambda i: (i, 0)), pl.BlockSpec(memory_space=pl.ANY)],
        out_specs=pl.BlockSpec(memory_space=pl.ANY),
        out_shape=jax.ShapeDtypeStruct((n_rows, ROW_WORDS), F32),
        scratch_shapes=[pltpu.SemaphoreType.DMA(())],
        input_output_aliases={2: 0},
        compiler_params=_cparams(("arbitrary",)),
        name="moe_dispatch",
    )(slots3, rows, jnp.zeros((n_rows, ROW_WORDS), F32))


def _expert_body(elo_ref, ehi_ref, used_ref, rows_ref, wg0_ref, wu0_ref, wd0_ref, wg1_ref, wu1_ref, wd1_ref,
                 g_ref, b_ref, z_ref, wg_sc, wu_sc, wd_sc):
    r = pl.program_id(0)
    used = used_ref[r]
    prev = jnp.maximum(r - 1, 0)

    @pl.when((r == 0) | (elo_ref[r] != elo_ref[prev]))
    def _():
        wg_sc[0] = wg0_ref[0].astype(BF16)
        wu_sc[0] = wu0_ref[0].astype(BF16)
        wd_sc[0] = wd0_ref[0].astype(BF16)

    @pl.when((r == 0) | (ehi_ref[r] != ehi_ref[prev]))
    def _():
        wg_sc[1] = wg1_ref[0].astype(BF16)
        wu_sc[1] = wu1_ref[0].astype(BF16)
        wd_sc[1] = wd1_ref[0].astype(BF16)

    @pl.when(used > 0)
    def _():
        h1 = rows_ref[:, :D_MODEL]
        meta = rows_ref[:, D_MODEL:]
        lane = lax.broadcasted_iota(jnp.int32, (1, LANES), 1).astype(F32)
        x = h1.astype(BF16)
        ffn = jnp.zeros((TE, D_MODEL), F32)
        for side, lane_w in ((0, M_WLO), (1, M_WHI)):
            g = jnp.dot(x, wg_sc[side], preferred_element_type=F32)
            u = jnp.dot(x, wu_sc[side], preferred_element_type=F32)
            hid = (g * (1.0 / (1.0 + jnp.exp(-g))) * u).astype(BF16)
            y = jnp.dot(hid, wd_sc[side], preferred_element_type=F32)
            ffn = ffn + _lane_pick(meta, lane, lane_w) * y
        z_ref[...] = _layer_norm(ALPHA * h1 + ffn, g_ref[...], b_ref[...])

    @pl.when(used <= 0)
    def _():
        z_ref[...] = jnp.zeros(z_ref.shape, F32)


def _expert_mlp(tile_elo, tile_ehi, tile_used, xs, w_gate, w_up, w_down, ln_g, ln_b):
    n_rows = xs.shape[0]
    n_tiles = n_rows // TE
    up_spec = lambda which: pl.BlockSpec((1, D_MODEL, EXPERT_FF), lambda r, lo, hi, us: ((lo, hi)[which][r], 0, 0))
    down_spec = lambda which: pl.BlockSpec((1, EXPERT_FF, D_MODEL), lambda r, lo, hi, us: ((lo, hi)[which][r], 0, 0))
    row = pl.BlockSpec((1, D_MODEL), lambda r, lo, hi, us: (0, 0))
    grid_spec = pltpu.PrefetchScalarGridSpec(
        num_scalar_prefetch=3,
        grid=(n_tiles,),
        in_specs=[pl.BlockSpec((TE, ROW_WORDS), lambda r, lo, hi, us: (r, 0)),
                  up_spec(0), up_spec(0), down_spec(0), up_spec(1), up_spec(1), down_spec(1), row, row],
        out_specs=pl.BlockSpec((TE, D_MODEL), lambda r, lo, hi, us: (r, 0)),
        scratch_shapes=[pltpu.VMEM((2, D_MODEL, EXPERT_FF), BF16), pltpu.VMEM((2, D_MODEL, EXPERT_FF), BF16),
                        pltpu.VMEM((2, EXPERT_FF, D_MODEL), BF16)],
    )
    return pl.pallas_call(
        _expert_body,
        grid_spec=grid_spec,
        out_shape=jax.ShapeDtypeStruct((n_rows, D_MODEL), F32),
        compiler_params=_cparams(("arbitrary",)),
        name="expert_mlp",
    )(tile_elo, tile_ehi, tile_used, xs, w_gate, w_up, w_down, w_gate, w_up, w_down, ln_g, ln_b)


def _undispatch_body(tm, slots_ref, z_ref, o_ref, sem):
    def row_copy(j):
        return pltpu.make_async_copy(z_ref.at[pl.ds(slots_ref[0, 0, j], 1), :], o_ref.at[pl.ds(j, 1), :], sem)

    def start(j, carry):
        row_copy(j).start()
        return carry

    def wait(j, carry):
        row_copy(j).wait()
        return carry

    lax.fori_loop(0, tm, start, 0, unroll=8)
    lax.fori_loop(0, tm, wait, 0, unroll=8)


def _undispatch(slots3, z):
    tm = slots3.shape[2]
    t = slots3.shape[0] * tm
    return pl.pallas_call(
        functools.partial(_undispatch_body, tm),
        grid=(t // tm,),
        in_specs=[pl.BlockSpec((1, 1, tm), lambda i: (i, 0, 0), memory_space=pltpu.SMEM),
                  pl.BlockSpec(memory_space=pl.ANY)],
        out_specs=pl.BlockSpec((tm, D_MODEL), lambda i: (i, 0)),
        out_shape=jax.ShapeDtypeStruct((t, D_MODEL), F32),
        scratch_shapes=[pltpu.SemaphoreType.DMA(())],
        compiler_params=_cparams(("arbitrary",)),
        name="moe_undispatch",
    )(slots3, z)


def _slot_blocks(slots, tm):
    t = slots.shape[0]
    return slots[:, 0].reshape(t // tm, 1, tm)


def kernel(x, positions, ln_emb_g, ln_emb_b, w_in, a_sink, q_a_norm_g, w_q_b, kv_a_norm_g, w_kv_b, out_norm_a_g, out_norm_b_g, w_out, ln_attn_g, ln_attn_b, w_group, b_group, w_expert, b_expert, w_gate, w_up, w_down, ln_ffn_g, ln_ffn_b):
    batch, seq, d = x.shape
    t = batch * seq
    x2 = x.reshape(t, d)
    row = lambda a: a.reshape(1, -1).astype(F32)

    tables = _rope_tables(positions)
    w1, wva, wq, wk, wv = _proj_weights(w_in[0], w_q_b[0], w_kv_b[0])
    qa, ka, vat, qb, kb, vt = _projections(x2, row(ln_emb_g), row(ln_emb_b), w1, wva, wq, wk, wv,
                                           row(q_a_norm_g[0]), row(kv_a_norm_g[0]), tables)
    oa = _attention_a(qa, ka, vat, a_sink[0].reshape(1, A_HEADS).astype(F32), batch, seq)
    ob = _attention_b(qb, kb, vt, batch, seq)

    wo = w_out[0]
    woa = wo[:A_Q_COLS].astype(BF16)
    wob = wo[A_Q_COLS:].astype(BF16)
    gb = row(out_norm_b_g[0])
    wr = jnp.concatenate([w_expert[0], w_group[0],
                          jnp.zeros((d, LANES - N_EXPERTS - N_GROUPS), F32)], -1)
    wr_hi = wr.astype(BF16)
    wr_lo = (wr - wr_hi.astype(F32)).astype(BF16)
    br = jnp.concatenate([b_expert[0], b_group[0], jnp.zeros((LANES - N_EXPERTS - N_GROUPS,), F32)]).reshape(1, LANES)
    consts = [row(out_norm_a_g[0]), gb, woa, wob, row(ln_emb_g), row(ln_emb_b), row(ln_attn_g[0]), row(ln_attn_b[0]),
              wr_hi, wr_lo, br]
    rows, route, counts = _outproj_router(oa, ob, x2, consts)

    n_tiles = -(-t // TE) + N_CLASSES
    n_tiles = -(-n_tiles // 8) * 8
    slots, tmap = _slot_assign(route, counts, n_tiles)
    slots3 = _slot_blocks(slots, min(TM_DISP, t))
    xs = _dispatch(slots3, rows, n_tiles * TE)
    z = _expert_mlp(tmap[:, T_ELO], tmap[:, T_EHI], tmap[:, T_USED], xs, w_gate[0], w_up[0], w_down[0],
                    row(ln_ffn_g[0]), row(ln_ffn_b[0]))
    return _undispatch(slots3, z).reshape(batch, seq, d)
```

```python
import functools

import numpy as np
import jax
import jax.numpy as jnp
from jax import lax
from jax.experimental import pallas as pl
from jax.experimental.pallas import tpu as pltpu

F32 = jnp.float32
BF16 = jnp.bfloat16

D_MODEL = 1024
A_HEADS = 8
A_KV_HEADS = 2
A_HEAD_DIM = 64
WINDOW = 128
B_HEADS = 8
B_NOPE = 64
B_ROPE = 32
B_V = 64
B_Q_RANK = 768
B_KV_RANK = 256
ROPE_THETA = 10000.0
N_GROUPS = 4
EXPERTS_PER_GROUP = 8
N_EXPERTS = N_GROUPS * EXPERTS_PER_GROUP
EXPERT_FF = 256
LN_EPS = 1e-5
RMS_EPS = 1e-6
DEPTH = 1
ALPHA = (2 * DEPTH) ** 0.25
NEG_BIG = -1e30
LOG2_E = 1.4426950408889634

LANES = 128
A_Q_COLS = A_HEADS * A_HEAD_DIM
A_KV_COLS = A_KV_HEADS * A_HEAD_DIM
PAD_COLS = B_HEADS * LANES
BF16_SUBLANES = 16
VT_ROWS = B_V + BF16_SUBLANES
A_GROUP = A_HEADS // A_KV_HEADS

C_QA = 0
C_KA = C_QA + A_HEADS * LANES
C_CQ = C_KA + A_KV_COLS
C_CKV = C_CQ + B_Q_RANK
C_KR = C_CKV + B_KV_RANK
W1_COLS = C_KR + LANES

TM_IN_PROJ = 512
TM_PROJ = 256
TQ_A = 256
TQ_B = 1024
TK_B = 512
KT_PER_ITER = 16
TE = 384
TM_SLOT = 1024
TM_DISP = 1024
VMEM_LIMIT = 56 * 1024 * 1024


def _cparams(sem):
    return pltpu.CompilerParams(dimension_semantics=sem, vmem_limit_bytes=VMEM_LIMIT)


def _layer_norm(x, g, b):
    mu = jnp.mean(x, -1, keepdims=True)
    xc = x - mu
    var = jnp.mean(xc * xc, -1, keepdims=True)
    return xc * lax.rsqrt(var + LN_EPS) * g + b


def _rope_block(xc, cos, sin_signed, first_half, half):
    rot = jnp.where(first_half, pltpu.roll(xc, LANES - half, 1), pltpu.roll(xc, half, 1))
    return xc * cos + rot * sin_signed


HALF_A = A_HEAD_DIM // 2
HALF_B = B_ROPE // 2


def _rope_table_body(pos_ref, inv_ref, cosa_ref, sina_ref, cosb_ref, sinb_ref):
    ang = pos_ref[...].astype(F32) * inv_ref[...]
    cc = jnp.cos(ang)
    ss = jnp.sin(ang)
    lane = lax.broadcasted_iota(jnp.int32, (1, LANES), 1)

    def table_a(x):
        out = x
        for k in range(1, LANES // HALF_A):
            out = jnp.where(lane >= k * HALF_A, pltpu.roll(x, k * HALF_A, 1), out)
        return out

    def table_b(x, fill):
        lo = pltpu.roll(x, B_NOPE - HALF_A, 1)
        hi = pltpu.roll(x, B_NOPE - HALF_A + HALF_B, 1)
        out = jnp.where(lane < B_NOPE + HALF_B, lo, hi)
        return jnp.where((lane >= B_NOPE) & (lane < B_NOPE + B_ROPE), out, fill)

    sign_a = jnp.where(lane % A_HEAD_DIM < HALF_A, -1.0, 1.0)
    sign_b = jnp.where(lane < B_NOPE + HALF_B, -1.0, 1.0)
    cosa_ref[...] = table_a(cc)
    sina_ref[...] = table_a(ss) * sign_a
    cosb_ref[...] = table_b(cc, 1.0)
    sinb_ref[...] = table_b(ss, 0.0) * sign_b


def _rope_tables(positions):
    t = positions.shape[0] * positions.shape[1]
    tm = 512
    inv_a = 1.0 / (ROPE_THETA ** (np.arange(0, A_HEAD_DIM, 2, dtype=np.float64) / A_HEAD_DIM))
    inv_b = 1.0 / (ROPE_THETA ** (np.arange(0, B_ROPE, 2, dtype=np.float64) / B_ROPE))
    inv = np.zeros((1, LANES))
    inv[0, :HALF_A] = inv_a
    inv[0, HALF_A:HALF_A + HALF_B] = inv_b
    tab = pl.BlockSpec((tm, LANES), lambda i: (i, 0))
    return pl.pallas_call(
        _rope_table_body,
        grid=(t // tm,),
        in_specs=[pl.BlockSpec((tm, 1), lambda i: (i, 0)), pl.BlockSpec((1, LANES), lambda i: (0, 0))],
        out_specs=[tab, tab, tab, tab],
        out_shape=[jax.ShapeDtypeStruct((t, LANES), F32)] * 4,
        compiler_params=_cparams(("parallel",)),
        name="rope_tables",
    )(positions.reshape(t, 1), jnp.asarray(inv, F32))


def _ones_rows(vt):
    vrow = lax.broadcasted_iota(jnp.int32, (vt.shape[0], 1), 0)
    return jnp.where(vrow % VT_ROWS >= B_V, 1.0, vt)


def _proj_body(x_ref, g_ref, b_ref, w1_ref, wva_ref, wq_ref, wk_ref, wv_ref, gq_ref, gkv_ref,
               cosa_ref, sina_ref, cosb_ref, sinb_ref,
               qa_ref, ka_ref, vat_ref, qb_ref, kb_ref, vt_ref):
    h = _layer_norm(x_ref[...], g_ref[...], b_ref[...]).astype(BF16)
    lane = lax.broadcasted_iota(jnp.int32, (1, LANES), 1)
    first_a = (lane % A_HEAD_DIM) < (A_HEAD_DIM // 2)
    first_b = (lane % B_ROPE) < (B_ROPE // 2)
    cosa, sina = cosa_ref[...], sina_ref[...]
    cosb, sinb = cosb_ref[...], sinb_ref[...]

    def proj(c0, width):
        return jnp.dot(h, w1_ref[:, c0:c0 + width], preferred_element_type=F32)

    qa = proj(C_QA, A_HEADS * LANES)
    scale_a = A_HEAD_DIM ** -0.5 * LOG2_E
    for c in range(A_HEADS):
        blk = _rope_block(qa[:, c * LANES:(c + 1) * LANES], cosa, sina, first_a, HALF_A)
        qa_ref[:, c * LANES:(c + 1) * LANES] = (blk * scale_a).astype(BF16)
    ka_ref[...] = _rope_block(proj(C_KA, A_KV_COLS), cosa, sina, first_a, HALF_A).astype(BF16)
    vat = lax.dot_general(wva_ref[...], h, (((1,), (1,)), ((), ())), preferred_element_type=F32)
    vat_ref[...] = _ones_rows(vat).astype(BF16)

    cq = proj(C_CQ, B_Q_RANK)
    cq = cq * lax.rsqrt(jnp.mean(cq * cq, -1, keepdims=True) + RMS_EPS) * gq_ref[...]
    qb = jnp.dot(cq.astype(BF16), wq_ref[...], preferred_element_type=F32)
    scale_b = (B_NOPE + B_ROPE) ** -0.5 * LOG2_E
    for c in range(B_HEADS):
        blk = _rope_block(qb[:, c * LANES:(c + 1) * LANES], cosb, sinb, first_b, HALF_B)
        qb_ref[:, c * LANES:(c + 1) * LANES] = (blk * scale_b).astype(BF16)

    ckv = proj(C_CKV, B_KV_RANK)
    ckv = (ckv * lax.rsqrt(jnp.mean(ckv * ckv, -1, keepdims=True) + RMS_EPS) * gkv_ref[...]).astype(BF16)
    kr = _rope_block(proj(C_KR, LANES), cosb, sinb, first_b, HALF_B)
    kn = jnp.dot(ckv, wk_ref[...], preferred_element_type=F32)
    for c in range(B_HEADS):
        kb_ref[:, c * LANES:(c + 1) * LANES] = (kn[:, c * LANES:(c + 1) * LANES] + kr).astype(BF16)
    vt = lax.dot_general(wv_ref[...], ckv, (((1,), (1,)), ((), ())), preferred_element_type=F32)
    vt_ref[...] = _ones_rows(vt).astype(BF16)


def _proj_weights(w_in, w_q_b, w_kv_b):
    w = w_in
    d = w.shape[0]
    o = A_Q_COLS
    ka = w[:, o:o + A_KV_COLS]; o += A_KV_COLS
    va = w[:, o:o + A_KV_COLS]; o += A_KV_COLS
    cq = w[:, o:o + B_Q_RANK]; o += B_Q_RANK
    ckv = w[:, o:o + B_KV_RANK]; o += B_KV_RANK
    kr = w[:, o:o + B_ROPE]
    kr_blk = jnp.concatenate([jnp.zeros((d, B_NOPE), w.dtype), kr,
                              jnp.zeros((d, LANES - B_NOPE - B_ROPE), w.dtype)], -1)
    qa = w[:, :A_Q_COLS].reshape(d, A_KV_HEADS, A_GROUP, A_HEAD_DIM)
    qa = jnp.stack([jnp.pad(qa[:, g], ((0, 0), (0, 0), (g * A_HEAD_DIM, LANES - (g + 1) * A_HEAD_DIM)))
                    for g in range(A_KV_HEADS)], 1).reshape(d, A_HEADS * LANES)
    w1 = jnp.concatenate([qa, ka, cq, ckv, kr_blk], -1).astype(BF16)
    wva = jnp.pad(va.reshape(d, A_KV_HEADS, A_HEAD_DIM), ((0, 0), (0, 0), (0, VT_ROWS - A_HEAD_DIM)))
    wva = wva.reshape(d, A_KV_HEADS * VT_ROWS).T.astype(BF16)
    wq = w_q_b.reshape(B_Q_RANK, B_HEADS, B_NOPE + B_ROPE)
    wq = jnp.pad(wq, ((0, 0), (0, 0), (0, LANES - B_NOPE - B_ROPE))).reshape(B_Q_RANK, PAD_COLS).astype(BF16)
    wkv = w_kv_b.reshape(B_KV_RANK, B_HEADS, B_NOPE + B_V)
    wk = jnp.pad(wkv[:, :, :B_NOPE], ((0, 0), (0, 0), (0, LANES - B_NOPE))).reshape(B_KV_RANK, PAD_COLS).astype(BF16)
    wv = jnp.pad(wkv[:, :, B_NOPE:], ((0, 0), (0, 0), (0, VT_ROWS - B_V)))
    wv = wv.reshape(B_KV_RANK, B_HEADS * VT_ROWS).T.astype(BF16)
    return w1, wva, wq, wk, wv


def _projections(x2, ln_g, ln_b, w1, wva, wq, wk, wv, gq, gkv, tables):
    t = x2.shape[0]
    tm = TM_IN_PROJ
    full = lambda a: pl.BlockSpec(a.shape, lambda i: (0,) * a.ndim)
    tile = lambda n: pl.BlockSpec((tm, n), lambda i: (i, 0))
    consts = [ln_g, ln_b, w1, wva, wq, wk, wv, gq, gkv]
    tok = lambda n: (tile(n), jax.ShapeDtypeStruct((t, n), BF16))
    feat = lambda r: (pl.BlockSpec((r, tm), lambda i: (0, i)), jax.ShapeDtypeStruct((r, t), BF16))
    outs = [tok(A_HEADS * LANES), tok(A_KV_COLS), feat(A_KV_HEADS * VT_ROWS),
            tok(PAD_COLS), tok(PAD_COLS), feat(B_HEADS * VT_ROWS)]
    return pl.pallas_call(
        _proj_body,
        grid=(t // tm,),
        in_specs=[tile(D_MODEL)] + [full(a) for a in consts] + [tile(LANES)] * 4,
        out_specs=[o[0] for o in outs],
        out_shape=[o[1] for o in outs],
        compiler_params=_cparams(("parallel",)),
        name="ln_in_proj",
    )(x2, *consts, *tables)


def _attn_a_body(seq, sink_ref, q_ref, k0, k1, k2, k3, v0, v1, v2, v3, o_ref, s_sc):
    i = pl.program_id(1)
    kk = jnp.concatenate([k0[...], k1[...], k2[...], k3[...]], 0)
    vt = jnp.concatenate([v0[...], v1[...], v2[...], v3[...]], 1)
    nk = kk.shape[0]
    kpos = i * TQ_A - WINDOW + lax.broadcasted_iota(jnp.int32, (nk, 1), 0)
    qpos = i * TQ_A + lax.broadcasted_iota(jnp.int32, (1, TQ_A), 1)
    valid = (kpos >= 0) & (kpos < seq) & (jnp.abs(kpos - qpos) <= WINDOW)
    bias = jnp.where(valid, 0.0, NEG_BIG)

    def scores(h):
        qt = q_ref[:, h * LANES:(h + 1) * LANES].astype(F32).T.astype(BF16)
        s = jnp.dot(kk, qt, preferred_element_type=F32) + bias
        return s, jnp.max(s, 0, keepdims=True)

    s0, m_tile = scores(0)
    s_sc[0] = s0
    for h in range(A_HEADS):
        g = h // A_GROUP
        if h + 1 < A_HEADS:
            s_next, mt_next = scores(h + 1)
            s_sc[(h + 1) % 2] = s_next
        sink = sink_ref[0, h] * LOG2_E
        m = jnp.maximum(m_tile, sink)
        p = jnp.exp2(s_sc[h % 2] - m).astype(BF16)
        acc = jnp.dot(vt[g * VT_ROWS:(g + 1) * VT_ROWS, :], p, preferred_element_type=F32)
        den = acc[A_HEAD_DIM:A_HEAD_DIM + 1, :] + jnp.exp2(sink - m)
        o_ref[h * A_HEAD_DIM:(h + 1) * A_HEAD_DIM, :] = acc[:A_HEAD_DIM, :] / den
        if h + 1 < A_HEADS:
            m_tile = mt_next


def _attention_a(qa, ka, vat, sink, batch, seq):
    t = qa.shape[0]
    nq = seq // TQ_A
    nb = seq // WINDOW
    per = TQ_A // WINDOW
    assert per + 2 == 4

    def halo(j):
        return lambda b, i: b * nb + jnp.clip(per * i - 1 + j, 0, nb - 1)

    k_specs = [pl.BlockSpec((WINDOW, A_KV_COLS), lambda b, i, f=halo(j): (f(b, i), 0)) for j in range(per + 2)]
    v_specs = [pl.BlockSpec((A_KV_HEADS * VT_ROWS, WINDOW), lambda b, i, f=halo(j): (0, f(b, i)))
               for j in range(per + 2)]
    return pl.pallas_call(
        functools.partial(_attn_a_body, seq),
        grid=(batch, nq),
        in_specs=[pl.BlockSpec(memory_space=pltpu.SMEM),
                  pl.BlockSpec((TQ_A, A_HEADS * LANES), lambda b, i: (b * nq + i, 0))] + k_specs + v_specs,
        out_specs=pl.BlockSpec((A_Q_COLS, TQ_A), lambda b, i: (0, b * nq + i)),
        out_shape=jax.ShapeDtypeStruct((A_Q_COLS, t), F32),
        scratch_shapes=[pltpu.VMEM((2, TQ_A + 2 * WINDOW, TQ_A), F32)],
        compiler_params=_cparams(("parallel", "parallel")),
        name="window_attn",
    )(sink, qa, *([ka] * (per + 2)), *([vat] * (per + 2)))


def _attn_b_body(nkt, q_ref, k_ref, vt_ref, o_ref, s_sc, acc_sc):
    qt = q_ref[...].astype(F32).T.astype(BF16)
    acc_sc[...] = jnp.zeros(acc_sc.shape, F32)

    def scores(j):
        start = pl.multiple_of(j * TK_B, TK_B)
        s = jnp.dot(k_ref[pl.ds(start, TK_B), :], qt, preferred_element_type=F32)
        return s, jnp.max(s, 0, keepdims=True)

    def update(j, s, m, m_tile):
        start = pl.multiple_of(j * TK_B, TK_B)
        m_new = jnp.maximum(m, m_tile)
        alpha = jnp.exp2(m - m_new)
        p = jnp.exp2(s - m_new).astype(BF16)
        pv = jnp.dot(vt_ref[:, pl.ds(start, TK_B)], p, preferred_element_type=F32)
        acc_sc[...] = alpha * acc_sc[...] + pv
        return m_new

    s0, mt0 = scores(0)
    s_sc[0] = s0

    per_iter = min(KT_PER_ITER, nkt)
    assert nkt % per_iter == 0 and per_iter % 2 == 0

    def group(jj, carry):
        m, m_tile = carry
        for u in range(per_iter):
            j = per_iter * jj + u
            s_next, mt_next = scores(jnp.minimum(j + 1, nkt - 1))
            s_sc[(u + 1) % 2] = s_next
            m = update(j, s_sc[u % 2], m, m_tile)
            m_tile = mt_next
        return m, m_tile

    m0 = jnp.full((1, TQ_B), -jnp.inf, F32)
    lax.fori_loop(0, nkt // per_iter, group, (m0, mt0))
    o_ref[...] = acc_sc[:B_V, :] / acc_sc[B_V:B_V + 1, :]


def _attention_b(qb, kb, vt, batch, seq):
    t = qb.shape[0]
    nq = seq // TQ_B
    nkt = seq // TK_B
    return pl.pallas_call(
        functools.partial(_attn_b_body, nkt),
        grid=(batch * B_HEADS, nq),
        in_specs=[pl.BlockSpec((TQ_B, LANES), lambda bh, i: ((bh // B_HEADS) * nq + i, bh % B_HEADS)),
                  pl.BlockSpec((seq, LANES), lambda bh, i: (bh // B_HEADS, bh % B_HEADS)),
                  pl.BlockSpec((VT_ROWS, seq), lambda bh, i: (bh % B_HEADS, bh // B_HEADS))],
        out_specs=pl.BlockSpec((B_V, TQ_B), lambda bh, i: (bh % B_HEADS, (bh // B_HEADS) * nq + i)),
        out_shape=jax.ShapeDtypeStruct((B_HEADS * B_V, t), F32),
        scratch_shapes=[pltpu.VMEM((2, TK_B, TQ_B), F32), pltpu.VMEM((VT_ROWS, TQ_B), F32)],
        compiler_params=_cparams(("parallel", "arbitrary")),
        name="mla_attn",
    )(qb, kb, vt)


R_CLASS, R_RANK = range(2)
M_WLO, M_WHI = range(2)
GROUP_LANE0 = N_EXPERTS
PAIRS_PER_GROUP = EXPERTS_PER_GROUP * (EXPERTS_PER_GROUP - 1) // 2
N_CLASSES = N_GROUPS * PAIRS_PER_GROUP
ROW_WORDS = D_MODEL + LANES


def _lane_pick(slab, lane, idx):
    return jnp.sum(jnp.where(lane == idx, slab, 0.0), -1, keepdims=True)


def _outproj_body(oa_ref, ob_ref, x_ref, ga_ref, gb_ref, woa_ref, wob_ref, lg0_ref, lb0_ref, lg1_ref, lb1_ref,
                  wrh_ref, wrl_ref, br_ref, rows_ref, route_ref, counts_ref, carry_sc):
    @pl.when(pl.program_id(0) == 0)
    def _():
        carry_sc[...] = jnp.zeros(carry_sc.shape, F32)

    oa = oa_ref[...].T
    ob = ob_ref[...].T
    na = oa * lax.rsqrt(jnp.mean(oa * oa, -1, keepdims=True) + RMS_EPS) * ga_ref[...]
    nb = ob * lax.rsqrt(jnp.mean(ob * ob, -1, keepdims=True) + RMS_EPS) * gb_ref[...]
    mixed = (jnp.dot(na.astype(BF16), woa_ref[...], preferred_element_type=F32)
             + jnp.dot(nb.astype(BF16), wob_ref[...], preferred_element_type=F32))
    h0 = _layer_norm(x_ref[...], lg0_ref[...], lb0_ref[...])
    h1 = _layer_norm(ALPHA * h0 + mixed, lg1_ref[...], lb1_ref[...])
    rows_ref[:, :D_MODEL] = h1

    hi = h1.astype(BF16)
    lo = (h1 - hi.astype(F32)).astype(BF16)
    logits = (jnp.dot(hi, wrh_ref[...], preferred_element_type=F32)
              + jnp.dot(lo, wrh_ref[...], preferred_element_type=F32)
              + jnp.dot(hi, wrl_ref[...], preferred_element_type=F32)) + br_ref[...]

    tm = logits.shape[0]
    lane = lax.broadcasted_iota(jnp.int32, (1, LANES), 1).astype(F32)
    far = float(LANES)
    is_g = (lane >= GROUP_LANE0) & (lane < GROUP_LANE0 + N_GROUPS)
    gl = jnp.where(is_g, logits, -jnp.inf)
    gmax = jnp.max(gl, -1, keepdims=True)
    gidx = jnp.min(jnp.where(gl == gmax, lane, far), -1, keepdims=True) - GROUP_LANE0
    pg = 1.0 / jnp.sum(jnp.where(is_g, jnp.exp(jnp.where(is_g, logits, gmax) - gmax), 0.0), -1, keepdims=True)
    in_grp = (lane >= gidx * EXPERTS_PER_GROUP) & (lane < (gidx + 1.0) * EXPERTS_PER_GROUP)
    el = jnp.where(in_grp, logits, -jnp.inf)
    v1 = jnp.max(el, -1, keepdims=True)
    i1 = jnp.min(jnp.where(el == v1, lane, far), -1, keepdims=True)
    el2 = jnp.where(lane == i1, -jnp.inf, el)
    v2 = jnp.max(el2, -1, keepdims=True)
    i2 = jnp.min(jnp.where(el2 == v2, lane, far), -1, keepdims=True)
    ex = jnp.exp(v2 - v1)
    w1 = pg / (1.0 + ex)
    w2 = pg * ex / (1.0 + ex)

    first_lo = i1 < i2
    a = jnp.minimum(i1, i2) - gidx * EXPERTS_PER_GROUP
    b = jnp.maximum(i1, i2) - gidx * EXPERTS_PER_GROUP
    pair = a * (2.0 * EXPERTS_PER_GROUP - 1.0 - a) * 0.5 + (b - a - 1.0)
    cls = gidx * PAIRS_PER_GROUP + pair
    w_lo = jnp.where(first_lo, w1, w2)
    w_hi = jnp.where(first_lo, w2, w1)
    rows_ref[:, D_MODEL:] = jnp.where(lane == M_WLO, w_lo, jnp.where(lane == M_WHI, w_hi, 0.0))

    onehot = jnp.where(lane == cls, 1.0, 0.0)
    r_i = lax.broadcasted_iota(jnp.int32, (tm, tm), 0)
    c_i = lax.broadcasted_iota(jnp.int32, (tm, tm), 1)
    lower = jnp.where(c_i < r_i, 1.0, 0.0).astype(BF16)
    prefix = jnp.dot(lower, onehot.astype(BF16), preferred_element_type=F32) + carry_sc[0:1, :]
    rank = _lane_pick(prefix, lane, cls)
    new_carry = carry_sc[0:1, :] + jnp.sum(onehot, 0, keepdims=True)
    carry_sc[...] = jnp.broadcast_to(new_carry, carry_sc.shape)
    counts_ref[...] = jnp.broadcast_to(new_carry, counts_ref.shape)
    route_ref[...] = jnp.where(lane == R_CLASS, cls, jnp.where(lane == R_RANK, rank, 0.0))


def _outproj_router(oa, ob, x2, consts):
    t = x2.shape[0]
    tm = TM_PROJ
    full = lambda a: pl.BlockSpec(a.shape, lambda i: (0,) * a.ndim)
    tile = lambda n: pl.BlockSpec((tm, n), lambda i: (i, 0))
    return pl.pallas_call(
        _outproj_body,
        grid=(t // tm,),
        in_specs=[pl.BlockSpec((A_Q_COLS, tm), lambda i: (0, i)),
                  pl.BlockSpec((B_HEADS * B_V, tm), lambda i: (0, i)), tile(D_MODEL)]
        + [full(a) for a in consts],
        out_specs=[tile(ROW_WORDS), tile(LANES), pl.BlockSpec((8, LANES), lambda i: (0, 0))],
        out_shape=[jax.ShapeDtypeStruct((t, ROW_WORDS), F32),
                   jax.ShapeDtypeStruct((t, LANES), F32), jax.ShapeDtypeStruct((8, LANES), F32)],
        scratch_shapes=[pltpu.VMEM((8, LANES), F32)],
        compiler_params=_cparams(("arbitrary",)),
        name="out_proj_router",
    )(oa, ob, x2, *consts)


T_CLASS, T_USED, T_ELO, T_EHI = range(4)


def _slot_body(n_tiles, route_ref, counts_ref, elo_ref, ehi_ref, slots_ref, tmap_ref):
    lane_i = lax.broadcasted_iota(jnp.int32, (1, LANES), 1)
    lane = lane_i.astype(F32)
    is_c = lane_i < N_CLASSES
    cnt = jnp.where(is_c, counts_ref[0:1, :], 0.0)
    padded = jnp.floor((cnt + (TE - 1)) * (1.0 / TE)) * TE
    incl = padded
    k = 1
    while k < LANES:
        incl = incl + jnp.where(lane_i >= k, pltpu.roll(incl, k, 1), 0.0)
        k *= 2
    off = incl - padded

    route = route_ref[...]
    cls = _lane_pick(route, lane, R_CLASS)
    slot = _lane_pick(jnp.broadcast_to(off, route.shape), lane, cls) + _lane_pick(route, lane, R_RANK)
    slots_ref[...] = jnp.where(lane == 0, slot, 0.0).astype(jnp.int32)

    row0 = lax.broadcasted_iota(jnp.int32, (n_tiles, 1), 0).astype(F32) * TE
    tc = jnp.sum(jnp.where(is_c & (incl <= row0), 1.0, 0.0), -1, keepdims=True)
    tc = jnp.minimum(tc, N_CLASSES - 1.0)
    at_tc = lane == tc
    used = jnp.clip(jnp.sum(jnp.where(at_tc, cnt - (row0 - off), 0.0), -1, keepdims=True), 0.0, float(TE))
    elo = jnp.sum(jnp.where(at_tc, elo_ref[...], 0.0), -1, keepdims=True)
    ehi = jnp.sum(jnp.where(at_tc, ehi_ref[...], 0.0), -1, keepdims=True)
    tmap = jnp.zeros((n_tiles, LANES), F32)
    for idx, val in ((T_CLASS, tc), (T_USED, used), (T_ELO, elo), (T_EHI, ehi)):
        tmap = jnp.where(lane == idx, val, tmap)
    tmap_ref[...] = tmap.astype(jnp.int32)


def _class_experts():
    elo = np.zeros((1, LANES))
    ehi = np.zeros((1, LANES))
    c = 0
    for g in range(N_GROUPS):
        for a in range(EXPERTS_PER_GROUP):
            for b in range(a + 1, EXPERTS_PER_GROUP):
                elo[0, c] = g * EXPERTS_PER_GROUP + a
                ehi[0, c] = g * EXPERTS_PER_GROUP + b
                c += 1
    return jnp.asarray(elo, F32), jnp.asarray(ehi, F32)


def _slot_assign(route, counts, n_tiles):
    t = route.shape[0]
    tm = min(TM_SLOT, t)
    row = pl.BlockSpec((1, LANES), lambda i: (0, 0))
    return pl.pallas_call(
        functools.partial(_slot_body, n_tiles),
        grid=(t // tm,),
        in_specs=[pl.BlockSpec((tm, LANES), lambda i: (i, 0)), pl.BlockSpec((8, LANES), lambda i: (0, 0)), row, row],
        out_specs=[pl.BlockSpec((tm, LANES), lambda i: (i, 0)), pl.BlockSpec((n_tiles, LANES), lambda i: (0, 0))],
        out_shape=[jax.ShapeDtypeStruct((t, LANES), jnp.int32), jax.ShapeDtypeStruct((n_tiles, LANES), jnp.int32)],
        compiler_params=_cparams(("arbitrary",)),
        name="slot_assign",
    )(route, counts, *_class_experts())


def _dispatch_body(tm, slots_ref, src_ref, zeros_ref, dst_ref, sem):
    del zeros_ref

    def row_copy(i, u):
        return pltpu.make_async_copy(src_ref.at[i, pl.ds(u, 1), :],
                                     dst_ref.at[pl.ds(slots_ref[i * ROW_GROUP + u], 1), :], sem)

    _row_dma_loops(tm, row_copy)


def _dispatch(slots, rows, n_rows):
    t = rows.shape[0]
    tm = min(TM_DISP, t)
    return pl.pallas_call(
        functools.partial(_dispatch_body, tm),
        grid=(t // tm,),
        in_specs=[pl.BlockSpec((tm,), lambda i: (i,), memory_space=pltpu.SMEM),
                  pl.BlockSpec((tm // ROW_GROUP, ROW_GROUP, ROW_WORDS), lambda i: (i, 0, 0)),
                  pl.BlockSpec(memory_space=pl.ANY)],
        out_specs=pl.BlockSpec(memory_space=pl.ANY),
        out_shape=jax.ShapeDtypeStruct((n_rows, ROW_WORDS), F32),
        scratch_shapes=[pltpu.SemaphoreType.DMA(())],
        input_output_aliases={2: 0},
        compiler_params=_cparams(("arbitrary",)),
        name="moe_dispatch",
    )(slots, rows.reshape(t // ROW_GROUP, ROW_GROUP, ROW_WORDS), jnp.zeros((n_rows, ROW_WORDS), F32))


def _expert_body(elo_ref, ehi_ref, used_ref, src_ref, rows_ref, wg0_ref, wu0_ref, wd0_ref, wg1_ref, wu1_ref, wd1_ref,
                 g_ref, b_ref, z_ref, wg_sc, wu_sc, wd_sc):
    del src_ref
    r = pl.program_id(0)
    used = used_ref[r]
    prev = jnp.maximum(r - 1, 0)

    @pl.when((r == 0) | (elo_ref[r] != elo_ref[prev]))
    def _():
        wg_sc[0] = wg0_ref[0].astype(BF16)
        wu_sc[0] = wu0_ref[0].astype(BF16)
        wd_sc[0] = wd0_ref[0].astype(BF16)

    @pl.when((r == 0) | (ehi_ref[r] != ehi_ref[prev]))
    def _():
        wg_sc[1] = wg1_ref[0].astype(BF16)
        wu_sc[1] = wu1_ref[0].astype(BF16)
        wd_sc[1] = wd1_ref[0].astype(BF16)

    @pl.when(used > 0)
    def _():
        h1 = rows_ref[:, :D_MODEL]
        meta = rows_ref[:, D_MODEL:]
        lane = lax.broadcasted_iota(jnp.int32, (1, LANES), 1).astype(F32)
        x = h1.astype(BF16)
        ffn = jnp.zeros((TE, D_MODEL), F32)
        for side, lane_w in ((0, M_WLO), (1, M_WHI)):
            g = jnp.dot(x, wg_sc[side], preferred_element_type=F32)
            u = jnp.dot(x, wu_sc[side], preferred_element_type=F32)
            hid = (g * (1.0 / (1.0 + jnp.exp(-g))) * u).astype(BF16)
            y = jnp.dot(hid, wd_sc[side], preferred_element_type=F32)
            ffn = ffn + _lane_pick(meta, lane, lane_w) * y
        z_ref[...] = _layer_norm(ALPHA * h1 + ffn, g_ref[...], b_ref[...])

    @pl.when(used <= 0)
    def _():
        z_ref[...] = jnp.zeros(z_ref.shape, F32)


def _expert_mlp(tile_elo, tile_ehi, tile_used, xs, w_gate, w_up, w_down, ln_g, ln_b):
    n_rows = xs.shape[0]
    n_tiles = n_rows // TE
    last_used = jnp.maximum(jnp.sum((tile_used > 0).astype(jnp.int32)) - 1, 0)
    tile_src = jnp.minimum(jnp.arange(n_tiles, dtype=jnp.int32), last_used)
    up_spec = lambda which: pl.BlockSpec((1, D_MODEL, EXPERT_FF),
                                         lambda r, lo, hi, us, src: ((lo, hi)[which][r], 0, 0))
    down_spec = lambda which: pl.BlockSpec((1, EXPERT_FF, D_MODEL),
                                           lambda r, lo, hi, us, src: ((lo, hi)[which][r], 0, 0))
    row = pl.BlockSpec((1, D_MODEL), lambda r, lo, hi, us, src: (0, 0))
    grid_spec = pltpu.PrefetchScalarGridSpec(
        num_scalar_prefetch=4,
        grid=(n_tiles,),
        in_specs=[pl.BlockSpec((TE, ROW_WORDS), lambda r, lo, hi, us, src: (src[r], 0)),
                  up_spec(0), up_spec(0), down_spec(0), up_spec(1), up_spec(1), down_spec(1), row, row],
        out_specs=pl.BlockSpec((TE, D_MODEL), lambda r, lo, hi, us, src: (r, 0)),
        scratch_shapes=[pltpu.VMEM((2, D_MODEL, EXPERT_FF), BF16), pltpu.VMEM((2, D_MODEL, EXPERT_FF), BF16),
                        pltpu.VMEM((2, EXPERT_FF, D_MODEL), BF16)],
    )
    return pl.pallas_call(
        _expert_body,
        grid_spec=grid_spec,
        out_shape=jax.ShapeDtypeStruct((n_rows, D_MODEL), F32),
        compiler_params=_cparams(("arbitrary",)),
        name="expert_mlp",
    )(tile_elo, tile_ehi, tile_used, tile_src, xs, w_gate, w_up, w_down, w_gate, w_up, w_down, ln_g, ln_b)


ROW_GROUP = 8


def _row_dma_loops(tm, row_copy):
    def start(i, carry):
        for u in range(ROW_GROUP):
            row_copy(i, u).start()
        return carry

    def wait(i, carry):
        for u in range(ROW_GROUP):
            row_copy(i, u).wait()
        return carry

    lax.fori_loop(0, tm // ROW_GROUP, start, 0)
    lax.fori_loop(0, tm // ROW_GROUP, wait, 0)


def _undispatch_body(tm, slots_ref, z_ref, o_ref, sem):
    def row_copy(i, u):
        return pltpu.make_async_copy(z_ref.at[pl.ds(slots_ref[i * ROW_GROUP + u], 1), :],
                                     o_ref.at[i, pl.ds(u, 1), :], sem)

    _row_dma_loops(tm, row_copy)


def _undispatch(slots, z):
    t = slots.shape[0]
    tm = min(TM_DISP, t)
    return pl.pallas_call(
        functools.partial(_undispatch_body, tm),
        grid=(t // tm,),
        in_specs=[pl.BlockSpec((tm,), lambda i: (i,), memory_space=pltpu.SMEM),
                  pl.BlockSpec(memory_space=pl.ANY)],
        out_specs=pl.BlockSpec((tm // ROW_GROUP, ROW_GROUP, D_MODEL), lambda i: (i, 0, 0)),
        out_shape=jax.ShapeDtypeStruct((t // ROW_GROUP, ROW_GROUP, D_MODEL), F32),
        scratch_shapes=[pltpu.SemaphoreType.DMA(())],
        compiler_params=_cparams(("arbitrary",)),
        name="moe_undispatch",
    )(slots, z)


def kernel(x, positions, ln_emb_g, ln_emb_b, w_in, a_sink, q_a_norm_g, w_q_b, kv_a_norm_g, w_kv_b, out_norm_a_g, out_norm_b_g, w_out, ln_attn_g, ln_attn_b, w_group, b_group, w_expert, b_expert, w_gate, w_up, w_down, ln_ffn_g, ln_ffn_b):
    batch, seq, d = x.shape
    t = batch * seq
    x2 = x.reshape(t, d)
    row = lambda a: a.reshape(1, -1).astype(F32)

    tables = _rope_tables(positions)
    w1, wva, wq, wk, wv = _proj_weights(w_in[0], w_q_b[0], w_kv_b[0])
    qa, ka, vat, qb, kb, vt = _projections(x2, row(ln_emb_g), row(ln_emb_b), w1, wva, wq, wk, wv,
                                           row(q_a_norm_g[0]), row(kv_a_norm_g[0]), tables)
    oa = _attention_a(qa, ka, vat, a_sink[0].reshape(1, A_HEADS).astype(F32), batch, seq)
    ob = _attention_b(qb, kb, vt, batch, seq)

    wo = w_out[0]
    woa = wo[:A_Q_COLS].astype(BF16)
    wob = wo[A_Q_COLS:].astype(BF16)
    gb = row(out_norm_b_g[0])
    wr = jnp.concatenate([w_expert[0], w_group[0],
                          jnp.zeros((d, LANES - N_EXPERTS - N_GROUPS), F32)], -1)
    wr_hi = wr.astype(BF16)
    wr_lo = (wr - wr_hi.astype(F32)).astype(BF16)
    br = jnp.concatenate([b_expert[0], b_group[0], jnp.zeros((LANES - N_EXPERTS - N_GROUPS,), F32)]).reshape(1, LANES)
    consts = [row(out_norm_a_g[0]), gb, woa, wob, row(ln_emb_g), row(ln_emb_b), row(ln_attn_g[0]), row(ln_attn_b[0]),
              wr_hi, wr_lo, br]
    rows, route, counts = _outproj_router(oa, ob, x2, consts)

    n_tiles = -(-t // TE) + N_CLASSES
    n_tiles = -(-n_tiles // 8) * 8
    slots, tmap = _slot_assign(route, counts, n_tiles)
    slot = slots[:, 0]
    xs = _dispatch(slot, rows, n_tiles * TE)
    z = _expert_mlp(tmap[:, T_ELO], tmap[:, T_EHI], tmap[:, T_USED], xs, w_gate[0], w_up[0], w_down[0],
                    row(ln_ffn_g[0]), row(ln_ffn_b[0]))
    return _undispatch(slot, z).reshape(batch, seq, d)
```

```python
import functools

import numpy as np
import jax
import jax.numpy as jnp
from jax import lax
from jax.experimental import pallas as pl
from jax.experimental.pallas import tpu as pltpu

F32 = jnp.float32
BF16 = jnp.bfloat16

D_MODEL = 1024
A_HEADS = 8
A_KV_HEADS = 2
A_HEAD_DIM = 64
WINDOW = 128
B_HEADS = 8
B_NOPE = 64
B_ROPE = 32
B_V = 64
B_Q_RANK = 768
B_KV_RANK = 256
ROPE_THETA = 10000.0
N_GROUPS = 4
EXPERTS_PER_GROUP = 8
N_EXPERTS = N_GROUPS * EXPERTS_PER_GROUP
EXPERT_FF = 256
LN_EPS = 1e-5
RMS_EPS = 1e-6
DEPTH = 1
ALPHA = (2 * DEPTH) ** 0.25
NEG_BIG = -1e30
LOG2_E = 1.4426950408889634

LANES = 128
A_Q_COLS = A_HEADS * A_HEAD_DIM
A_KV_COLS = A_KV_HEADS * A_HEAD_DIM
PAD_COLS = B_HEADS * LANES
BF16_SUBLANES = 16
VT_ROWS = B_V + BF16_SUBLANES
A_GROUP = A_HEADS // A_KV_HEADS

C_QA = 0
C_KA = C_QA + A_HEADS * LANES
C_CQ = C_KA + A_KV_COLS
C_CKV = C_CQ + B_Q_RANK
C_KR = C_CKV + B_KV_RANK
W1_COLS = C_KR + LANES

TM_IN_PROJ = 512
TM_PROJ = 256
TQ_A = 256
TQ_B = 1024
TK_B = 512
KT_PER_ITER = 16
TE = 384
TM_SLOT = 1024
TM_DISP = 1024
VMEM_LIMIT = 56 * 1024 * 1024


def _cparams(sem):
    return pltpu.CompilerParams(dimension_semantics=sem, vmem_limit_bytes=VMEM_LIMIT)


def _layer_norm(x, g, b):
    mu = jnp.mean(x, -1, keepdims=True)
    xc = x - mu
    var = jnp.mean(xc * xc, -1, keepdims=True)
    return xc * lax.rsqrt(var + LN_EPS) * g + b


def _rope_block(xc, cos, sin_signed, first_half, half):
    rot = jnp.where(first_half, pltpu.roll(xc, LANES - half, 1), pltpu.roll(xc, half, 1))
    return xc * cos + rot * sin_signed


HALF_A = A_HEAD_DIM // 2
HALF_B = B_ROPE // 2


def _rope_values(pos, inv):
    ang = pos.astype(F32) * inv
    cc = jnp.cos(ang)
    ss = jnp.sin(ang)
    lane = lax.broadcasted_iota(jnp.int32, (1, LANES), 1)

    def table_a(x):
        out = x
        for k in range(1, LANES // HALF_A):
            out = jnp.where(lane >= k * HALF_A, pltpu.roll(x, k * HALF_A, 1), out)
        return out

    def table_b(x, fill):
        lo = pltpu.roll(x, B_NOPE - HALF_A, 1)
        hi = pltpu.roll(x, B_NOPE - HALF_A + HALF_B, 1)
        out = jnp.where(lane < B_NOPE + HALF_B, lo, hi)
        return jnp.where((lane >= B_NOPE) & (lane < B_NOPE + B_ROPE), out, fill)

    sign_a = jnp.where(lane % A_HEAD_DIM < HALF_A, -1.0, 1.0)
    sign_b = jnp.where(lane < B_NOPE + HALF_B, -1.0, 1.0)
    return table_a(cc), table_a(ss) * sign_a, table_b(cc, 1.0), table_b(ss, 0.0) * sign_b


def _rope_rates():
    inv_a = 1.0 / (ROPE_THETA ** (np.arange(0, A_HEAD_DIM, 2, dtype=np.float64) / A_HEAD_DIM))
    inv_b = 1.0 / (ROPE_THETA ** (np.arange(0, B_ROPE, 2, dtype=np.float64) / B_ROPE))
    inv = np.zeros((1, LANES))
    inv[0, :HALF_A] = inv_a
    inv[0, HALF_A:HALF_A + HALF_B] = inv_b
    return jnp.asarray(inv, F32)


def _ones_rows(vt):
    vrow = lax.broadcasted_iota(jnp.int32, (vt.shape[0], 1), 0)
    return jnp.where(vrow % VT_ROWS >= B_V, 1.0, vt)


def _proj_body(x_ref, pos_ref, g_ref, b_ref, w1_ref, wva_ref, wq_ref, wk_ref, wv_ref, gq_ref, gkv_ref, inv_ref,
               h0_ref, qa_ref, ka_ref, vat_ref, qb_ref, kb_ref, vt_ref):
    h0 = _layer_norm(x_ref[...], g_ref[...], b_ref[...])
    h0_ref[...] = h0
    h = h0.astype(BF16)
    lane = lax.broadcasted_iota(jnp.int32, (1, LANES), 1)
    first_a = (lane % A_HEAD_DIM) < (A_HEAD_DIM // 2)
    first_b = (lane % B_ROPE) < (B_ROPE // 2)
    cosa, sina, cosb, sinb = _rope_values(pos_ref[...], inv_ref[...])

    def proj(c0, width):
        return jnp.dot(h, w1_ref[:, c0:c0 + width], preferred_element_type=F32)

    qa = proj(C_QA, A_HEADS * LANES)
    scale_a = A_HEAD_DIM ** -0.5 * LOG2_E
    for c in range(A_HEADS):
        blk = _rope_block(qa[:, c * LANES:(c + 1) * LANES], cosa, sina, first_a, HALF_A)
        qa_ref[:, c * LANES:(c + 1) * LANES] = (blk * scale_a).astype(BF16)
    ka_ref[...] = _rope_block(proj(C_KA, A_KV_COLS), cosa, sina, first_a, HALF_A).astype(BF16)
    vat = lax.dot_general(wva_ref[...], h, (((1,), (1,)), ((), ())), preferred_element_type=F32)
    vat_ref[...] = _ones_rows(vat).astype(BF16)

    cq = proj(C_CQ, B_Q_RANK)
    cq = cq * lax.rsqrt(jnp.mean(cq * cq, -1, keepdims=True) + RMS_EPS) * gq_ref[...]
    qb = jnp.dot(cq.astype(BF16), wq_ref[...], preferred_element_type=F32)
    scale_b = (B_NOPE + B_ROPE) ** -0.5 * LOG2_E
    for c in range(B_HEADS):
        blk = _rope_block(qb[:, c * LANES:(c + 1) * LANES], cosb, sinb, first_b, HALF_B)
        qb_ref[:, c * LANES:(c + 1) * LANES] = (blk * scale_b).astype(BF16)

    ckv = proj(C_CKV, B_KV_RANK)
    ckv = (ckv * lax.rsqrt(jnp.mean(ckv * ckv, -1, keepdims=True) + RMS_EPS) * gkv_ref[...]).astype(BF16)
    kr = _rope_block(proj(C_KR, LANES), cosb, sinb, first_b, HALF_B)
    kn = jnp.dot(ckv, wk_ref[...], preferred_element_type=F32)
    for c in range(B_HEADS):
        kb_ref[:, c * LANES:(c + 1) * LANES] = (kn[:, c * LANES:(c + 1) * LANES] + kr).astype(BF16)
    vt = lax.dot_general(wv_ref[...], ckv, (((1,), (1,)), ((), ())), preferred_element_type=F32)
    vt_ref[...] = _ones_rows(vt).astype(BF16)


def _proj_weights(w_in, w_q_b, w_kv_b):
    w = w_in
    d = w.shape[0]
    o = A_Q_COLS
    ka = w[:, o:o + A_KV_COLS]; o += A_KV_COLS
    va = w[:, o:o + A_KV_COLS]; o += A_KV_COLS
    cq = w[:, o:o + B_Q_RANK]; o += B_Q_RANK
    ckv = w[:, o:o + B_KV_RANK]; o += B_KV_RANK
    kr = w[:, o:o + B_ROPE]
    kr_blk = jnp.concatenate([jnp.zeros((d, B_NOPE), w.dtype), kr,
                              jnp.zeros((d, LANES - B_NOPE - B_ROPE), w.dtype)], -1)
    qa = w[:, :A_Q_COLS].reshape(d, A_KV_HEADS, A_GROUP, A_HEAD_DIM)
    qa = jnp.stack([jnp.pad(qa[:, g], ((0, 0), (0, 0), (g * A_HEAD_DIM, LANES - (g + 1) * A_HEAD_DIM)))
                    for g in range(A_KV_HEADS)], 1).reshape(d, A_HEADS * LANES)
    w1 = jnp.concatenate([qa, ka, cq, ckv, kr_blk], -1).astype(BF16)
    wva = jnp.pad(va.reshape(d, A_KV_HEADS, A_HEAD_DIM), ((0, 0), (0, 0), (0, VT_ROWS - A_HEAD_DIM)))
    wva = wva.reshape(d, A_KV_HEADS * VT_ROWS).T.astype(BF16)
    wq = w_q_b.reshape(B_Q_RANK, B_HEADS, B_NOPE + B_ROPE)
    wq = jnp.pad(wq, ((0, 0), (0, 0), (0, LANES - B_NOPE - B_ROPE))).reshape(B_Q_RANK, PAD_COLS).astype(BF16)
    wkv = w_kv_b.reshape(B_KV_RANK, B_HEADS, B_NOPE + B_V)
    wk = jnp.pad(wkv[:, :, :B_NOPE], ((0, 0), (0, 0), (0, LANES - B_NOPE))).reshape(B_KV_RANK, PAD_COLS).astype(BF16)
    wv = jnp.pad(wkv[:, :, B_NOPE:], ((0, 0), (0, 0), (0, VT_ROWS - B_V)))
    wv = wv.reshape(B_KV_RANK, B_HEADS * VT_ROWS).T.astype(BF16)
    return w1, wva, wq, wk, wv


def _projections(x2, pos, ln_g, ln_b, w1, wva, wq, wk, wv, gq, gkv):
    t = x2.shape[0]
    tm = TM_IN_PROJ
    full = lambda a: pl.BlockSpec(a.shape, lambda i: (0,) * a.ndim)
    tile = lambda n: pl.BlockSpec((tm, n), lambda i: (i, 0))
    consts = [ln_g, ln_b, w1, wva, wq, wk, wv, gq, gkv, _rope_rates()]
    tok = lambda n, dt=BF16: (tile(n), jax.ShapeDtypeStruct((t, n), dt))
    feat = lambda r: (pl.BlockSpec((r, tm), lambda i: (0, i)), jax.ShapeDtypeStruct((r, t), BF16))
    outs = [tok(D_MODEL, F32), tok(A_HEADS * LANES), tok(A_KV_COLS), feat(A_KV_HEADS * VT_ROWS),
            tok(PAD_COLS), tok(PAD_COLS), feat(B_HEADS * VT_ROWS)]
    return pl.pallas_call(
        _proj_body,
        grid=(t // tm,),
        in_specs=[tile(D_MODEL), tile(1)] + [full(a) for a in consts],
        out_specs=[o[0] for o in outs],
        out_shape=[o[1] for o in outs],
        compiler_params=_cparams(("parallel",)),
        name="ln_in_proj",
    )(x2, pos, *consts)


def _attn_a_body(seq, sink_ref, q_ref, k0, k1, k2, k3, v0, v1, v2, v3, o_ref, s_sc):
    i = pl.program_id(1)
    kk = jnp.concatenate([k0[...], k1[...], k2[...], k3[...]], 0)
    vt = jnp.concatenate([v0[...], v1[...], v2[...], v3[...]], 1)
    nk = kk.shape[0]
    kpos = i * TQ_A - WINDOW + lax.broadcasted_iota(jnp.int32, (nk, 1), 0)
    qpos = i * TQ_A + lax.broadcasted_iota(jnp.int32, (1, TQ_A), 1)
    valid = (kpos >= 0) & (kpos < seq) & (jnp.abs(kpos - qpos) <= WINDOW)
    bias = jnp.where(valid, 0.0, NEG_BIG)

    def scores(h):
        qt = q_ref[:, h * LANES:(h + 1) * LANES].astype(F32).T.astype(BF16)
        s = jnp.dot(kk, qt, preferred_element_type=F32) + bias
        return s, jnp.max(s, 0, keepdims=True)

    s0, m_tile = scores(0)
    s_sc[0] = s0
    for h in range(A_HEADS):
        g = h // A_GROUP
        if h + 1 < A_HEADS:
            s_next, mt_next = scores(h + 1)
            s_sc[(h + 1) % 2] = s_next
        sink = sink_ref[0, h] * LOG2_E
        m = jnp.maximum(m_tile, sink)
        p = jnp.exp2(s_sc[h % 2] - m).astype(BF16)
        acc = jnp.dot(vt[g * VT_ROWS:(g + 1) * VT_ROWS, :], p, preferred_element_type=F32)
        den = acc[A_HEAD_DIM:A_HEAD_DIM + 1, :] + jnp.exp2(sink - m)
        o_ref[h * A_HEAD_DIM:(h + 1) * A_HEAD_DIM, :] = acc[:A_HEAD_DIM, :] / den
        if h + 1 < A_HEADS:
            m_tile = mt_next


def _attention_a(qa, ka, vat, sink, batch, seq):
    t = qa.shape[0]
    nq = seq // TQ_A
    nb = seq // WINDOW
    per = TQ_A // WINDOW
    assert per + 2 == 4

    def halo(j):
        return lambda b, i: b * nb + jnp.clip(per * i - 1 + j, 0, nb - 1)

    k_specs = [pl.BlockSpec((WINDOW, A_KV_COLS), lambda b, i, f=halo(j): (f(b, i), 0)) for j in range(per + 2)]
    v_specs = [pl.BlockSpec((A_KV_HEADS * VT_ROWS, WINDOW), lambda b, i, f=halo(j): (0, f(b, i)))
               for j in range(per + 2)]
    return pl.pallas_call(
        functools.partial(_attn_a_body, seq),
        grid=(batch, nq),
        in_specs=[pl.BlockSpec(memory_space=pltpu.SMEM),
                  pl.BlockSpec((TQ_A, A_HEADS * LANES), lambda b, i: (b * nq + i, 0))] + k_specs + v_specs,
        out_specs=pl.BlockSpec((A_Q_COLS, TQ_A), lambda b, i: (0, b * nq + i)),
        out_shape=jax.ShapeDtypeStruct((A_Q_COLS, t), F32),
        scratch_shapes=[pltpu.VMEM((2, TQ_A + 2 * WINDOW, TQ_A), F32)],
        compiler_params=_cparams(("parallel", "parallel")),
        name="window_attn",
    )(sink, qa, *([ka] * (per + 2)), *([vat] * (per + 2)))


def _attn_b_body(nkt, q_ref, k_ref, vt_ref, o_ref, s_sc, acc_sc):
    qt = q_ref[...].astype(F32).T.astype(BF16)
    acc_sc[...] = jnp.zeros(acc_sc.shape, F32)

    def scores(j):
        start = pl.multiple_of(j * TK_B, TK_B)
        s = jnp.dot(k_ref[pl.ds(start, TK_B), :], qt, preferred_element_type=F32)
        return s, jnp.max(s, 0, keepdims=True)

    def update(j, s, m, m_tile):
        start = pl.multiple_of(j * TK_B, TK_B)
        m_new = jnp.maximum(m, m_tile)
        alpha = jnp.exp2(m - m_new)
        p = jnp.exp2(s - m_new).astype(BF16)
        pv = jnp.dot(vt_ref[:, pl.ds(start, TK_B)], p, preferred_element_type=F32)
        acc_sc[...] = alpha * acc_sc[...] + pv
        return m_new

    s0, mt0 = scores(0)
    s_sc[0] = s0

    per_iter = min(KT_PER_ITER, nkt)
    assert nkt % per_iter == 0 and per_iter % 2 == 0

    def group(jj, carry):
        m, m_tile = carry
        for u in range(per_iter):
            j = per_iter * jj + u
            s_next, mt_next = scores(jnp.minimum(j + 1, nkt - 1))
            s_sc[(u + 1) % 2] = s_next
            m = update(j, s_sc[u % 2], m, m_tile)
            m_tile = mt_next
        return m, m_tile

    m0 = jnp.full((1, TQ_B), -jnp.inf, F32)
    lax.fori_loop(0, nkt // per_iter, group, (m0, mt0))
    o_ref[...] = acc_sc[:B_V, :] / acc_sc[B_V:B_V + 1, :]


def _attention_b(qb, kb, vt, batch, seq):
    t = qb.shape[0]
    nq = seq // TQ_B
    nkt = seq // TK_B
    return pl.pallas_call(
        functools.partial(_attn_b_body, nkt),
        grid=(batch * B_HEADS, nq),
        in_specs=[pl.BlockSpec((TQ_B, LANES), lambda bh, i: ((bh // B_HEADS) * nq + i, bh % B_HEADS)),
                  pl.BlockSpec((seq, LANES), lambda bh, i: (bh // B_HEADS, bh % B_HEADS)),
                  pl.BlockSpec((VT_ROWS, seq), lambda bh, i: (bh % B_HEADS, bh // B_HEADS))],
        out_specs=pl.BlockSpec((B_V, TQ_B), lambda bh, i: (bh % B_HEADS, (bh // B_HEADS) * nq + i)),
        out_shape=jax.ShapeDtypeStruct((B_HEADS * B_V, t), F32),
        scratch_shapes=[pltpu.VMEM((2, TK_B, TQ_B), F32), pltpu.VMEM((VT_ROWS, TQ_B), F32)],
        compiler_params=_cparams(("parallel", "arbitrary")),
        name="mla_attn",
    )(qb, kb, vt)


R_CLASS, R_RANK = range(2)
M_WLO, M_WHI = range(2)
GROUP_LANE0 = N_EXPERTS
PAIRS_PER_GROUP = EXPERTS_PER_GROUP * (EXPERTS_PER_GROUP - 1) // 2
N_CLASSES = N_GROUPS * PAIRS_PER_GROUP
ROW_WORDS = D_MODEL + LANES


def _lane_pick(slab, lane, idx):
    return jnp.sum(jnp.where(lane == idx, slab, 0.0), -1, keepdims=True)


def _outproj_body(oa_ref, ob_ref, h0_ref, ga_ref, gb_ref, woa_ref, wob_ref, lg1_ref, lb1_ref,
                  wrh_ref, wrl_ref, br_ref, rows_ref, route_ref, counts_ref, carry_sc):
    @pl.when(pl.program_id(0) == 0)
    def _():
        carry_sc[...] = jnp.zeros(carry_sc.shape, F32)

    oa = oa_ref[...].T
    ob = ob_ref[...].T
    na = oa * lax.rsqrt(jnp.mean(oa * oa, -1, keepdims=True) + RMS_EPS) * ga_ref[...]
    nb = ob * lax.rsqrt(jnp.mean(ob * ob, -1, keepdims=True) + RMS_EPS) * gb_ref[...]
    mixed = (jnp.dot(na.astype(BF16), woa_ref[...], preferred_element_type=F32)
             + jnp.dot(nb.astype(BF16), wob_ref[...], preferred_element_type=F32))
    h1 = _layer_norm(ALPHA * h0_ref[...] + mixed, lg1_ref[...], lb1_ref[...])
    rows_ref[:, :D_MODEL] = h1

    hi = h1.astype(BF16)
    lo = (h1 - hi.astype(F32)).astype(BF16)
    logits = (jnp.dot(hi, wrh_ref[...], preferred_element_type=F32)
              + jnp.dot(lo, wrh_ref[...], preferred_element_type=F32)
              + jnp.dot(hi, wrl_ref[...], preferred_element_type=F32)) + br_ref[...]

    tm = logits.shape[0]
    lane = lax.broadcasted_iota(jnp.int32, (1, LANES), 1).astype(F32)
    far = float(LANES)
    is_g = (lane >= GROUP_LANE0) & (lane < GROUP_LANE0 + N_GROUPS)
    gl = jnp.where(is_g, logits, -jnp.inf)
    gmax = jnp.max(gl, -1, keepdims=True)
    gidx = jnp.min(jnp.where(gl == gmax, lane, far), -1, keepdims=True) - GROUP_LANE0
    pg = 1.0 / jnp.sum(jnp.where(is_g, jnp.exp(jnp.where(is_g, logits, gmax) - gmax), 0.0), -1, keepdims=True)
    in_grp = (lane >= gidx * EXPERTS_PER_GROUP) & (lane < (gidx + 1.0) * EXPERTS_PER_GROUP)
    el = jnp.where(in_grp, logits, -jnp.inf)
    v1 = jnp.max(el, -1, keepdims=True)
    i1 = jnp.min(jnp.where(el == v1, lane, far), -1, keepdims=True)
    el2 = jnp.where(lane == i1, -jnp.inf, el)
    v2 = jnp.max(el2, -1, keepdims=True)
    i2 = jnp.min(jnp.where(el2 == v2, lane, far), -1, keepdims=True)
    ex = jnp.exp(v2 - v1)
    w1 = pg / (1.0 + ex)
    w2 = pg * ex / (1.0 + ex)

    first_lo = i1 < i2
    a = jnp.minimum(i1, i2) - gidx * EXPERTS_PER_GROUP
    b = jnp.maximum(i1, i2) - gidx * EXPERTS_PER_GROUP
    pair = a * (2.0 * EXPERTS_PER_GROUP - 1.0 - a) * 0.5 + (b - a - 1.0)
    cls = gidx * PAIRS_PER_GROUP + pair
    w_lo = jnp.where(first_lo, w1, w2)
    w_hi = jnp.where(first_lo, w2, w1)
    rows_ref[:, D_MODEL:] = jnp.where(lane == M_WLO, w_lo, jnp.where(lane == M_WHI, w_hi, 0.0))

    onehot = jnp.where(lane == cls, 1.0, 0.0)
    r_i = lax.broadcasted_iota(jnp.int32, (tm, tm), 0)
    c_i = lax.broadcasted_iota(jnp.int32, (tm, tm), 1)
    lower = jnp.where(c_i < r_i, 1.0, 0.0).astype(BF16)
    prefix = jnp.dot(lower, onehot.astype(BF16), preferred_element_type=F32) + carry_sc[0:1, :]
    rank = _lane_pick(prefix, lane, cls)
    new_carry = carry_sc[0:1, :] + jnp.sum(onehot, 0, keepdims=True)
    carry_sc[...] = jnp.broadcast_to(new_carry, carry_sc.shape)
    counts_ref[...] = jnp.broadcast_to(new_carry, counts_ref.shape)
    route_ref[...] = jnp.where(lane == R_CLASS, cls, jnp.where(lane == R_RANK, rank, 0.0))


def _outproj_router(oa, ob, x2, consts):
    t = x2.shape[0]
    tm = TM_PROJ
    full = lambda a: pl.BlockSpec(a.shape, lambda i: (0,) * a.ndim)
    tile = lambda n: pl.BlockSpec((tm, n), lambda i: (i, 0))
    return pl.pallas_call(
        _outproj_body,
        grid=(t // tm,),
        in_specs=[pl.BlockSpec((A_Q_COLS, tm), lambda i: (0, i)),
                  pl.BlockSpec((B_HEADS * B_V, tm), lambda i: (0, i)), tile(D_MODEL)]
        + [full(a) for a in consts],
        out_specs=[tile(ROW_WORDS), tile(LANES), pl.BlockSpec((8, LANES), lambda i: (0, 0))],
        out_shape=[jax.ShapeDtypeStruct((t, ROW_WORDS), F32),
                   jax.ShapeDtypeStruct((t, LANES), F32), jax.ShapeDtypeStruct((8, LANES), F32)],
        scratch_shapes=[pltpu.VMEM((8, LANES), F32)],
        compiler_params=_cparams(("arbitrary",)),
        name="out_proj_router",
    )(oa, ob, x2, *consts)


T_CLASS, T_USED, T_ELO, T_EHI = range(4)


def _slot_body(n_tiles, route_ref, counts_ref, elo_ref, ehi_ref, slots_ref, tmap_ref):
    lane_i = lax.broadcasted_iota(jnp.int32, (1, LANES), 1)
    lane = lane_i.astype(F32)
    is_c = lane_i < N_CLASSES
    cnt = jnp.where(is_c, counts_ref[0:1, :], 0.0)
    padded = jnp.floor((cnt + (TE - 1)) * (1.0 / TE)) * TE
    incl = padded
    k = 1
    while k < LANES:
        incl = incl + jnp.where(lane_i >= k, pltpu.roll(incl, k, 1), 0.0)
        k *= 2
    off = incl - padded

    route = route_ref[...]
    cls = _lane_pick(route, lane, R_CLASS)
    slot = _lane_pick(jnp.broadcast_to(off, route.shape), lane, cls) + _lane_pick(route, lane, R_RANK)
    slots_ref[...] = jnp.where(lane == 0, slot, 0.0).astype(jnp.int32)

    row0 = lax.broadcasted_iota(jnp.int32, (n_tiles, 1), 0).astype(F32) * TE
    tc = jnp.sum(jnp.where(is_c & (incl <= row0), 1.0, 0.0), -1, keepdims=True)
    tc = jnp.minimum(tc, N_CLASSES - 1.0)
    at_tc = lane == tc
    used = jnp.clip(jnp.sum(jnp.where(at_tc, cnt - (row0 - off), 0.0), -1, keepdims=True), 0.0, float(TE))
    elo = jnp.sum(jnp.where(at_tc, elo_ref[...], 0.0), -1, keepdims=True)
    ehi = jnp.sum(jnp.where(at_tc, ehi_ref[...], 0.0), -1, keepdims=True)
    tmap = jnp.zeros((n_tiles, LANES), F32)
    for idx, val in ((T_CLASS, tc), (T_USED, used), (T_ELO, elo), (T_EHI, ehi)):
        tmap = jnp.where(lane == idx, val, tmap)
    tmap_ref[...] = tmap.astype(jnp.int32)


def _class_experts():
    elo = np.zeros((1, LANES))
    ehi = np.zeros((1, LANES))
    c = 0
    for g in range(N_GROUPS):
        for a in range(EXPERTS_PER_GROUP):
            for b in range(a + 1, EXPERTS_PER_GROUP):
                elo[0, c] = g * EXPERTS_PER_GROUP + a
                ehi[0, c] = g * EXPERTS_PER_GROUP + b
                c += 1
    return jnp.asarray(elo, F32), jnp.asarray(ehi, F32)


def _slot_assign(route, counts, n_tiles):
    t = route.shape[0]
    tm = min(TM_SLOT, t)
    row = pl.BlockSpec((1, LANES), lambda i: (0, 0))
    return pl.pallas_call(
        functools.partial(_slot_body, n_tiles),
        grid=(t // tm,),
        in_specs=[pl.BlockSpec((tm, LANES), lambda i: (i, 0)), pl.BlockSpec((8, LANES), lambda i: (0, 0)), row, row],
        out_specs=[pl.BlockSpec((tm, LANES), lambda i: (i, 0)), pl.BlockSpec((n_tiles, LANES), lambda i: (0, 0))],
        out_shape=[jax.ShapeDtypeStruct((t, LANES), jnp.int32), jax.ShapeDtypeStruct((n_tiles, LANES), jnp.int32)],
        compiler_params=_cparams(("arbitrary",)),
        name="slot_assign",
    )(route, counts, *_class_experts())


def _dispatch_body(tm, slots_ref, src_ref, zeros_ref, dst_ref, sem):
    del zeros_ref

    def row_copy(i, u):
        return pltpu.make_async_copy(src_ref.at[i, pl.ds(u, 1), :],
                                     dst_ref.at[pl.ds(slots_ref[i * ROW_GROUP + u], 1), :], sem)

    _row_dma_loops(tm, row_copy)


def _dispatch(slots, rows, n_rows):
    t = rows.shape[0]
    tm = min(TM_DISP, t)
    return pl.pallas_call(
        functools.partial(_dispatch_body, tm),
        grid=(t // tm,),
        in_specs=[pl.BlockSpec((tm,), lambda i: (i,), memory_space=pltpu.SMEM),
                  pl.BlockSpec((tm // ROW_GROUP, ROW_GROUP, ROW_WORDS), lambda i: (i, 0, 0)),
                  pl.BlockSpec(memory_space=pl.ANY)],
        out_specs=pl.BlockSpec(memory_space=pl.ANY),
        out_shape=jax.ShapeDtypeStruct((n_rows, ROW_WORDS), F32),
        scratch_shapes=[pltpu.SemaphoreType.DMA(())],
        input_output_aliases={2: 0},
        compiler_params=_cparams(("arbitrary",)),
        name="moe_dispatch",
    )(slots, rows.reshape(t // ROW_GROUP, ROW_GROUP, ROW_WORDS), jnp.zeros((n_rows, ROW_WORDS), F32))


def _expert_body(elo_ref, ehi_ref, used_ref, src_ref, rows_ref, wg0_ref, wu0_ref, wd0_ref, wg1_ref, wu1_ref, wd1_ref,
                 g_ref, b_ref, z_ref, wg_sc, wu_sc, wd_sc):
    del src_ref
    r = pl.program_id(0)
    used = used_ref[r]
    prev = jnp.maximum(r - 1, 0)

    @pl.when((r == 0) | (elo_ref[r] != elo_ref[prev]))
    def _():
        wg_sc[0] = wg0_ref[0].astype(BF16)
        wu_sc[0] = wu0_ref[0].astype(BF16)
        wd_sc[0] = wd0_ref[0].astype(BF16)

    @pl.when((r == 0) | (ehi_ref[r] != ehi_ref[prev]))
    def _():
        wg_sc[1] = wg1_ref[0].astype(BF16)
        wu_sc[1] = wu1_ref[0].astype(BF16)
        wd_sc[1] = wd1_ref[0].astype(BF16)

    @pl.when(used > 0)
    def _():
        h1 = rows_ref[:, :D_MODEL]
        meta = rows_ref[:, D_MODEL:]
        lane = lax.broadcasted_iota(jnp.int32, (1, LANES), 1).astype(F32)
        x = h1.astype(BF16)
        ffn = jnp.zeros((TE, D_MODEL), F32)
        for side, lane_w in ((0, M_WLO), (1, M_WHI)):
            g = jnp.dot(x, wg_sc[side], preferred_element_type=F32)
            u = jnp.dot(x, wu_sc[side], preferred_element_type=F32)
            hid = (g * (1.0 / (1.0 + jnp.exp(-g))) * u).astype(BF16)
            y = jnp.dot(hid, wd_sc[side], preferred_element_type=F32)
            ffn = ffn + _lane_pick(meta, lane, lane_w) * y
        z_ref[...] = _layer_norm(ALPHA * h1 + ffn, g_ref[...], b_ref[...])

    @pl.when(used <= 0)
    def _():
        z_ref[...] = jnp.zeros(z_ref.shape, F32)


def _expert_mlp(tile_elo, tile_ehi, tile_used, xs, w_gate, w_up, w_down, ln_g, ln_b):
    n_rows = xs.shape[0]
    n_tiles = n_rows // TE
    last_used = jnp.maximum(jnp.sum((tile_used > 0).astype(jnp.int32)) - 1, 0)
    tile_src = jnp.minimum(jnp.arange(n_tiles, dtype=jnp.int32), last_used)
    up_spec = lambda which: pl.BlockSpec((1, D_MODEL, EXPERT_FF),
                                         lambda r, lo, hi, us, src: ((lo, hi)[which][r], 0, 0))
    down_spec = lambda which: pl.BlockSpec((1, EXPERT_FF, D_MODEL),
                                           lambda r, lo, hi, us, src: ((lo, hi)[which][r], 0, 0))
    row = pl.BlockSpec((1, D_MODEL), lambda r, lo, hi, us, src: (0, 0))
    grid_spec = pltpu.PrefetchScalarGridSpec(
        num_scalar_prefetch=4,
        grid=(n_tiles,),
        in_specs=[pl.BlockSpec((TE, ROW_WORDS), lambda r, lo, hi, us, src: (src[r], 0)),
                  up_spec(0), up_spec(0), down_spec(0), up_spec(1), up_spec(1), down_spec(1), row, row],
        out_specs=pl.BlockSpec((TE, D_MODEL), lambda r, lo, hi, us, src: (r, 0)),
        scratch_shapes=[pltpu.VMEM((2, D_MODEL, EXPERT_FF), BF16), pltpu.VMEM((2, D_MODEL, EXPERT_FF), BF16),
                        pltpu.VMEM((2, EXPERT_FF, D_MODEL), BF16)],
    )
    return pl.pallas_call(
        _expert_body,
        grid_spec=grid_spec,
        out_shape=jax.ShapeDtypeStruct((n_rows, D_MODEL), F32),
        compiler_params=_cparams(("arbitrary",)),
        name="expert_mlp",
    )(tile_elo, tile_ehi, tile_used, tile_src, xs, w_gate, w_up, w_down, w_gate, w_up, w_down, ln_g, ln_b)


ROW_GROUP = 8


def _row_dma_loops(tm, row_copy):
    def start(i, carry):
        for u in range(ROW_GROUP):
            row_copy(i, u).start()
        return carry

    def wait(i, carry):
        for u in range(ROW_GROUP):
            row_copy(i, u).wait()
        return carry

    lax.fori_loop(0, tm // ROW_GROUP, start, 0)
    lax.fori_loop(0, tm // ROW_GROUP, wait, 0)


def _undispatch_body(tm, slots_ref, z_ref, o_ref, sem):
    def row_copy(i, u):
        return pltpu.make_async_copy(z_ref.at[pl.ds(slots_ref[i * ROW_GROUP + u], 1), :],
                                     o_ref.at[i, pl.ds(u, 1), :], sem)

    _row_dma_loops(tm, row_copy)


def _undispatch(slots, z):
    t = slots.shape[0]
    tm = min(TM_DISP, t)
    return pl.pallas_call(
        functools.partial(_undispatch_body, tm),
        grid=(t // tm,),
        in_specs=[pl.BlockSpec((tm,), lambda i: (i,), memory_space=pltpu.SMEM),
                  pl.BlockSpec(memory_space=pl.ANY)],
        out_specs=pl.BlockSpec((tm // ROW_GROUP, ROW_GROUP, D_MODEL), lambda i: (i, 0, 0)),
        out_shape=jax.ShapeDtypeStruct((t // ROW_GROUP, ROW_GROUP, D_MODEL), F32),
        scratch_shapes=[pltpu.SemaphoreType.DMA(())],
        compiler_params=_cparams(("arbitrary",)),
        name="moe_undispatch",
    )(slots, z)


def kernel(x, positions, ln_emb_g, ln_emb_b, w_in, a_sink, q_a_norm_g, w_q_b, kv_a_norm_g, w_kv_b, out_norm_a_g, out_norm_b_g, w_out, ln_attn_g, ln_attn_b, w_group, b_group, w_expert, b_expert, w_gate, w_up, w_down, ln_ffn_g, ln_ffn_b):
    batch, seq, d = x.shape
    t = batch * seq
    x2 = x.reshape(t, d)
    row = lambda a: a.reshape(1, -1).astype(F32)

    w1, wva, wq, wk, wv = _proj_weights(w_in[0], w_q_b[0], w_kv_b[0])
    h0, qa, ka, vat, qb, kb, vt = _projections(x2, positions.reshape(t, 1), row(ln_emb_g), row(ln_emb_b),
                                               w1, wva, wq, wk, wv, row(q_a_norm_g[0]), row(kv_a_norm_g[0]))
    oa = _attention_a(qa, ka, vat, a_sink[0].reshape(1, A_HEADS).astype(F32), batch, seq)
    ob = _attention_b(qb, kb, vt, batch, seq)

    wo = w_out[0]
    woa = wo[:A_Q_COLS].astype(BF16)
    wob = wo[A_Q_COLS:].astype(BF16)
    gb = row(out_norm_b_g[0])
    wr = jnp.concatenate([w_expert[0], w_group[0],
                          jnp.zeros((d, LANES - N_EXPERTS - N_GROUPS), F32)], -1)
    wr_hi = wr.astype(BF16)
    wr_lo = (wr - wr_hi.astype(F32)).astype(BF16)
    br = jnp.concatenate([b_expert[0], b_group[0], jnp.zeros((LANES - N_EXPERTS - N_GROUPS,), F32)]).reshape(1, LANES)
    consts = [row(out_norm_a_g[0]), gb, woa, wob, row(ln_attn_g[0]), row(ln_attn_b[0]),
              wr_hi, wr_lo, br]
    rows, route, counts = _outproj_router(oa, ob, h0, consts)

    n_tiles = -(-t // TE) + N_CLASSES
    n_tiles = -(-n_tiles // 8) * 8
    slots, tmap = _slot_assign(route, counts, n_tiles)
    slot = slots[:, 0]
    xs = _dispatch(slot, rows, n_tiles * TE)
    z = _expert_mlp(tmap[:, T_ELO], tmap[:, T_EHI], tmap[:, T_USED], xs, w_gate[0], w_up[0], w_down[0],
                    row(ln_ffn_g[0]), row(ln_ffn_b[0]))
    return _undispatch(slot, z).reshape(batch, seq, d)
```

```python
import functools

import numpy as np
import jax
import jax.numpy as jnp
from jax import lax
from jax.experimental import pallas as pl
from jax.experimental.pallas import tpu as pltpu

F32 = jnp.float32
BF16 = jnp.bfloat16

D_MODEL = 1024
A_HEADS = 8
A_KV_HEADS = 2
A_HEAD_DIM = 64
WINDOW = 128
B_HEADS = 8
B_NOPE = 64
B_ROPE = 32
B_V = 64
B_Q_RANK = 768
B_KV_RANK = 256
ROPE_THETA = 10000.0
N_GROUPS = 4
EXPERTS_PER_GROUP = 8
N_EXPERTS = N_GROUPS * EXPERTS_PER_GROUP
EXPERT_FF = 256
LN_EPS = 1e-5
RMS_EPS = 1e-6
DEPTH = 1
ALPHA = (2 * DEPTH) ** 0.25
NEG_BIG = -1e30
LOG2_E = 1.4426950408889634

LANES = 128
A_Q_COLS = A_HEADS * A_HEAD_DIM
A_KV_COLS = A_KV_HEADS * A_HEAD_DIM
PAD_COLS = B_HEADS * LANES
BF16_SUBLANES = 16
VT_ROWS = B_V + BF16_SUBLANES
A_GROUP = A_HEADS // A_KV_HEADS

C_QA = 0
C_KA = C_QA + A_HEADS * LANES
C_CQ = C_KA + A_KV_COLS
C_CKV = C_CQ + B_Q_RANK
C_KR = C_CKV + B_KV_RANK
W1_COLS = C_KR + LANES

TM_IN_PROJ = 512
TM_PROJ = 256
TQ_A = 256
TQ_B = 1024
TK_B = 512
KT_PER_ITER = 16
TE = 384
TM_SLOT = 1024
TM_DISP = 1024
VMEM_LIMIT = 56 * 1024 * 1024


def _cparams(sem):
    return pltpu.CompilerParams(dimension_semantics=sem, vmem_limit_bytes=VMEM_LIMIT)


def _layer_norm(x, g, b):
    mu = jnp.mean(x, -1, keepdims=True)
    xc = x - mu
    var = jnp.mean(xc * xc, -1, keepdims=True)
    return xc * lax.rsqrt(var + LN_EPS) * g + b


def _rope_block(xc, cos, sin_signed, first_half, half):
    rot = jnp.where(first_half, pltpu.roll(xc, LANES - half, 1), pltpu.roll(xc, half, 1))
    return xc * cos + rot * sin_signed


HALF_A = A_HEAD_DIM // 2
HALF_B = B_ROPE // 2


def _rope_values(pos, inv):
    ang = pos.astype(F32) * inv
    cc = jnp.cos(ang)
    ss = jnp.sin(ang)
    lane = lax.broadcasted_iota(jnp.int32, (1, LANES), 1)

    def table_a(x):
        out = x
        for k in range(1, LANES // HALF_A):
            out = jnp.where(lane >= k * HALF_A, pltpu.roll(x, k * HALF_A, 1), out)
        return out

    def table_b(x, fill):
        lo = pltpu.roll(x, B_NOPE - HALF_A, 1)
        hi = pltpu.roll(x, B_NOPE - HALF_A + HALF_B, 1)
        out = jnp.where(lane < B_NOPE + HALF_B, lo, hi)
        return jnp.where((lane >= B_NOPE) & (lane < B_NOPE + B_ROPE), out, fill)

    sign_a = jnp.where(lane % A_HEAD_DIM < HALF_A, -1.0, 1.0)
    sign_b = jnp.where(lane < B_NOPE + HALF_B, -1.0, 1.0)
    return table_a(cc), table_a(ss) * sign_a, table_b(cc, 1.0), table_b(ss, 0.0) * sign_b


def _rope_rates():
    inv_a = 1.0 / (ROPE_THETA ** (np.arange(0, A_HEAD_DIM, 2, dtype=np.float64) / A_HEAD_DIM))
    inv_b = 1.0 / (ROPE_THETA ** (np.arange(0, B_ROPE, 2, dtype=np.float64) / B_ROPE))
    inv = np.zeros((1, LANES))
    inv[0, :HALF_A] = inv_a
    inv[0, HALF_A:HALF_A + HALF_B] = inv_b
    return jnp.asarray(inv, F32)


def _ones_rows(vt):
    vrow = lax.broadcasted_iota(jnp.int32, (vt.shape[0], 1), 0)
    return jnp.where(vrow % VT_ROWS >= B_V, 1.0, vt)


def _proj_body(x_ref, pos_ref, g_ref, b_ref, w1_ref, wva_ref, wq_ref, wk_ref, wv_ref, gq_ref, gkv_ref, inv_ref,
               h0_ref, qa_ref, ka_ref, vat_ref, qb_ref, kb_ref, vt_ref):
    lane = lax.broadcasted_iota(jnp.int32, (1, LANES), 1)
    first_a = (lane % A_HEAD_DIM) < (A_HEAD_DIM // 2)
    first_b = (lane % B_ROPE) < (B_ROPE // 2)
    scale_a = A_HEAD_DIM ** -0.5 * LOG2_E
    scale_b = (B_NOPE + B_ROPE) ** -0.5 * LOG2_E
    h0 = _layer_norm(x_ref[...], g_ref[...], b_ref[...])
    h0_ref[...] = h0
    h = h0.astype(BF16)
    cosa, sina, cosb, sinb = _rope_values(pos_ref[...], inv_ref[...])

    def proj(c0, width):
        return jnp.dot(h, w1_ref[:, c0:c0 + width], preferred_element_type=F32)

    qa = proj(C_QA, A_HEADS * LANES)
    for c in range(A_HEADS):
        blk = _rope_block(qa[:, c * LANES:(c + 1) * LANES], cosa, sina, first_a, HALF_A)
        qa_ref[:, c * LANES:(c + 1) * LANES] = (blk * scale_a).astype(BF16)
    ka_ref[...] = _rope_block(proj(C_KA, A_KV_COLS), cosa, sina, first_a, HALF_A).astype(BF16)
    vat = lax.dot_general(wva_ref[...], h, (((1,), (1,)), ((), ())), preferred_element_type=F32)
    vat_ref[...] = _ones_rows(vat).astype(BF16)

    cq = proj(C_CQ, B_Q_RANK)
    cq = cq * lax.rsqrt(jnp.mean(cq * cq, -1, keepdims=True) + RMS_EPS) * gq_ref[...]
    qb = jnp.dot(cq.astype(BF16), wq_ref[...], preferred_element_type=F32)
    for c in range(B_HEADS):
        blk = _rope_block(qb[:, c * LANES:(c + 1) * LANES], cosb, sinb, first_b, HALF_B)
        qb_ref[:, c * LANES:(c + 1) * LANES] = (blk * scale_b).astype(BF16)

    ckv = proj(C_CKV, B_KV_RANK)
    ckv = (ckv * lax.rsqrt(jnp.mean(ckv * ckv, -1, keepdims=True) + RMS_EPS) * gkv_ref[...]).astype(BF16)
    kr = _rope_block(proj(C_KR, LANES), cosb, sinb, first_b, HALF_B)
    kn = jnp.dot(ckv, wk_ref[...], preferred_element_type=F32)
    for c in range(B_HEADS):
        kb_ref[:, c * LANES:(c + 1) * LANES] = (kn[:, c * LANES:(c + 1) * LANES] + kr).astype(BF16)
    vt = lax.dot_general(wv_ref[...], ckv, (((1,), (1,)), ((), ())), preferred_element_type=F32)
    vt_ref[...] = _ones_rows(vt).astype(BF16)


def _proj_weights(w_in, w_q_b, w_kv_b):
    w = w_in
    d = w.shape[0]
    o = A_Q_COLS
    ka = w[:, o:o + A_KV_COLS]; o += A_KV_COLS
    va = w[:, o:o + A_KV_COLS]; o += A_KV_COLS
    cq = w[:, o:o + B_Q_RANK]; o += B_Q_RANK
    ckv = w[:, o:o + B_KV_RANK]; o += B_KV_RANK
    kr = w[:, o:o + B_ROPE]
    kr_blk = jnp.concatenate([jnp.zeros((d, B_NOPE), w.dtype), kr,
                              jnp.zeros((d, LANES - B_NOPE - B_ROPE), w.dtype)], -1)
    qa = w[:, :A_Q_COLS].reshape(d, A_KV_HEADS, A_GROUP, A_HEAD_DIM)
    qa = jnp.stack([jnp.pad(qa[:, g], ((0, 0), (0, 0), (g * A_HEAD_DIM, LANES - (g + 1) * A_HEAD_DIM)))
                    for g in range(A_KV_HEADS)], 1).reshape(d, A_HEADS * LANES)
    w1 = jnp.concatenate([qa, ka, cq, ckv, kr_blk], -1).astype(BF16)
    wva = jnp.pad(va.reshape(d, A_KV_HEADS, A_HEAD_DIM), ((0, 0), (0, 0), (0, VT_ROWS - A_HEAD_DIM)))
    wva = wva.reshape(d, A_KV_HEADS * VT_ROWS).T.astype(BF16)
    wq = w_q_b.reshape(B_Q_RANK, B_HEADS, B_NOPE + B_ROPE)
    wq = jnp.pad(wq, ((0, 0), (0, 0), (0, LANES - B_NOPE - B_ROPE))).reshape(B_Q_RANK, PAD_COLS).astype(BF16)
    wkv = w_kv_b.reshape(B_KV_RANK, B_HEADS, B_NOPE + B_V)
    wk = jnp.pad(wkv[:, :, :B_NOPE], ((0, 0), (0, 0), (0, LANES - B_NOPE))).reshape(B_KV_RANK, PAD_COLS).astype(BF16)
    wv = jnp.pad(wkv[:, :, B_NOPE:], ((0, 0), (0, 0), (0, VT_ROWS - B_V)))
    wv = wv.reshape(B_KV_RANK, B_HEADS * VT_ROWS).T.astype(BF16)
    return w1, wva, wq, wk, wv


def _projections(x2, pos, ln_g, ln_b, w1, wva, wq, wk, wv, gq, gkv):
    t = x2.shape[0]
    tm = TM_IN_PROJ
    full = lambda a: pl.BlockSpec(a.shape, lambda i: (0,) * a.ndim)
    tile = lambda n: pl.BlockSpec((tm, n), lambda i: (i, 0))
    consts = [ln_g, ln_b, w1, wva, wq, wk, wv, gq, gkv, _rope_rates()]
    tok = lambda n, dt=BF16: (tile(n), jax.ShapeDtypeStruct((t, n), dt))
    feat = lambda r: (pl.BlockSpec((r, tm), lambda i: (0, i)), jax.ShapeDtypeStruct((r, t), BF16))
    outs = [tok(D_MODEL, F32), tok(A_HEADS * LANES), tok(A_KV_COLS), feat(A_KV_HEADS * VT_ROWS),
            tok(PAD_COLS), tok(PAD_COLS), feat(B_HEADS * VT_ROWS)]
    return pl.pallas_call(
        _proj_body,
        grid=(t // tm,),
        in_specs=[tile(D_MODEL), tile(1)] + [full(a) for a in consts],
        out_specs=[o[0] for o in outs],
        out_shape=[o[1] for o in outs],
        compiler_params=_cparams(("parallel",)),
        name="ln_in_proj",
    )(x2, pos, *consts)


def _attn_a_body(seq, sink_ref, q_ref, k0, k1, k2, k3, v0, v1, v2, v3, o_ref, s_sc):
    i = pl.program_id(1)
    kk = jnp.concatenate([k0[...], k1[...], k2[...], k3[...]], 0)
    vt = jnp.concatenate([v0[...], v1[...], v2[...], v3[...]], 1)
    nk = kk.shape[0]
    kpos = i * TQ_A - WINDOW + lax.broadcasted_iota(jnp.int32, (nk, 1), 0)
    qpos = i * TQ_A + lax.broadcasted_iota(jnp.int32, (1, TQ_A), 1)
    valid = (kpos >= 0) & (kpos < seq) & (jnp.abs(kpos - qpos) <= WINDOW)
    bias = jnp.where(valid, 0.0, NEG_BIG)

    def scores(h):
        qt = q_ref[:, h * LANES:(h + 1) * LANES].astype(F32).T.astype(BF16)
        s = jnp.dot(kk, qt, preferred_element_type=F32) + bias
        return s, jnp.max(s, 0, keepdims=True)

    s0, m_tile = scores(0)
    s_sc[0] = s0
    for h in range(A_HEADS):
        g = h // A_GROUP
        if h + 1 < A_HEADS:
            s_next, mt_next = scores(h + 1)
            s_sc[(h + 1) % 2] = s_next
        sink = sink_ref[0, h] * LOG2_E
        m = jnp.maximum(m_tile, sink)
        p = jnp.exp2(s_sc[h % 2] - m).astype(BF16)
        acc = jnp.dot(vt[g * VT_ROWS:(g + 1) * VT_ROWS, :], p, preferred_element_type=F32)
        den = acc[A_HEAD_DIM:A_HEAD_DIM + 1, :] + jnp.exp2(sink - m)
        o_ref[h * A_HEAD_DIM:(h + 1) * A_HEAD_DIM, :] = acc[:A_HEAD_DIM, :] / den
        if h + 1 < A_HEADS:
            m_tile = mt_next


def _attention_a(qa, ka, vat, sink, batch, seq):
    t = qa.shape[0]
    nq = seq // TQ_A
    nb = seq // WINDOW
    per = TQ_A // WINDOW
    assert per + 2 == 4

    def halo(j):
        return lambda b, i: b * nb + jnp.clip(per * i - 1 + j, 0, nb - 1)

    k_specs = [pl.BlockSpec((WINDOW, A_KV_COLS), lambda b, i, f=halo(j): (f(b, i), 0)) for j in range(per + 2)]
    v_specs = [pl.BlockSpec((A_KV_HEADS * VT_ROWS, WINDOW), lambda b, i, f=halo(j): (0, f(b, i)))
               for j in range(per + 2)]
    return pl.pallas_call(
        functools.partial(_attn_a_body, seq),
        grid=(batch, nq),
        in_specs=[pl.BlockSpec(memory_space=pltpu.SMEM),
                  pl.BlockSpec((TQ_A, A_HEADS * LANES), lambda b, i: (b * nq + i, 0))] + k_specs + v_specs,
        out_specs=pl.BlockSpec((A_Q_COLS, TQ_A), lambda b, i: (0, b * nq + i)),
        out_shape=jax.ShapeDtypeStruct((A_Q_COLS, t), F32),
        scratch_shapes=[pltpu.VMEM((2, TQ_A + 2 * WINDOW, TQ_A), F32)],
        compiler_params=_cparams(("parallel", "parallel")),
        name="window_attn",
    )(sink, qa, *([ka] * (per + 2)), *([vat] * (per + 2)))


def _attn_b_body(nkt, q_ref, k_ref, vt_ref, o_ref, s_sc, acc_sc):
    qt = q_ref[...].astype(F32).T.astype(BF16)
    acc_sc[...] = jnp.zeros(acc_sc.shape, F32)

    def scores(j):
        start = pl.multiple_of(j * TK_B, TK_B)
        s = jnp.dot(k_ref[pl.ds(start, TK_B), :], qt, preferred_element_type=F32)
        return s, jnp.max(s, 0, keepdims=True)

    def update(j, s, m, m_tile):
        start = pl.multiple_of(j * TK_B, TK_B)
        m_new = jnp.maximum(m, m_tile)
        alpha = jnp.exp2(m - m_new)
        p = jnp.exp2(s - m_new).astype(BF16)
        pv = jnp.dot(vt_ref[:, pl.ds(start, TK_B)], p, preferred_element_type=F32)
        acc_sc[...] = alpha * acc_sc[...] + pv
        return m_new

    s0, mt0 = scores(0)
    s_sc[0] = s0

    per_iter = min(KT_PER_ITER, nkt)
    assert nkt % per_iter == 0 and per_iter % 2 == 0

    def group(jj, carry):
        m, m_tile = carry
        for u in range(per_iter):
            j = per_iter * jj + u
            s_next, mt_next = scores(jnp.minimum(j + 1, nkt - 1))
            s_sc[(u + 1) % 2] = s_next
            m = update(j, s_sc[u % 2], m, m_tile)
            m_tile = mt_next
        return m, m_tile

    m0 = jnp.full((1, TQ_B), -jnp.inf, F32)
    lax.fori_loop(0, nkt // per_iter, group, (m0, mt0))
    o_ref[...] = acc_sc[:B_V, :] / acc_sc[B_V:B_V + 1, :]


def _attention_b(qb, kb, vt, batch, seq):
    t = qb.shape[0]
    nq = seq // TQ_B
    nkt = seq // TK_B
    return pl.pallas_call(
        functools.partial(_attn_b_body, nkt),
        grid=(batch * B_HEADS, nq),
        in_specs=[pl.BlockSpec((TQ_B, LANES), lambda bh, i: ((bh // B_HEADS) * nq + i, bh % B_HEADS)),
                  pl.BlockSpec((seq, LANES), lambda bh, i: (bh // B_HEADS, bh % B_HEADS)),
                  pl.BlockSpec((VT_ROWS, seq), lambda bh, i: (bh % B_HEADS, bh // B_HEADS))],
        out_specs=pl.BlockSpec((B_V, TQ_B), lambda bh, i: (bh % B_HEADS, (bh // B_HEADS) * nq + i)),
        out_shape=jax.ShapeDtypeStruct((B_HEADS * B_V, t), F32),
        scratch_shapes=[pltpu.VMEM((2, TK_B, TQ_B), F32), pltpu.VMEM((VT_ROWS, TQ_B), F32)],
        compiler_params=_cparams(("parallel", "arbitrary")),
        name="mla_attn",
    )(qb, kb, vt)


R_CLASS, R_RANK = range(2)
M_WLO, M_WHI = range(2)
GROUP_LANE0 = N_EXPERTS
PAIRS_PER_GROUP = EXPERTS_PER_GROUP * (EXPERTS_PER_GROUP - 1) // 2
N_CLASSES = N_GROUPS * PAIRS_PER_GROUP
ROW_WORDS = D_MODEL + LANES


def _lane_pick(slab, lane, idx):
    return jnp.sum(jnp.where(lane == idx, slab, 0.0), -1, keepdims=True)


def _outproj_body(oa_ref, ob_ref, h0_ref, ga_ref, gb_ref, woa_ref, wob_ref, lg1_ref, lb1_ref,
                  wrh_ref, wrl_ref, br_ref, rows_ref, route_ref, counts_ref, carry_sc):
    @pl.when(pl.program_id(0) == 0)
    def _():
        carry_sc[...] = jnp.zeros(carry_sc.shape, F32)

    oa = oa_ref[...].T
    ob = ob_ref[...].T
    na = oa * lax.rsqrt(jnp.mean(oa * oa, -1, keepdims=True) + RMS_EPS) * ga_ref[...]
    nb = ob * lax.rsqrt(jnp.mean(ob * ob, -1, keepdims=True) + RMS_EPS) * gb_ref[...]
    mixed = (jnp.dot(na.astype(BF16), woa_ref[...], preferred_element_type=F32)
             + jnp.dot(nb.astype(BF16), wob_ref[...], preferred_element_type=F32))
    h1 = _layer_norm(ALPHA * h0_ref[...] + mixed, lg1_ref[...], lb1_ref[...])
    rows_ref[:, :D_MODEL] = h1

    hi = h1.astype(BF16)
    lo = (h1 - hi.astype(F32)).astype(BF16)
    logits = (jnp.dot(hi, wrh_ref[...], preferred_element_type=F32)
              + jnp.dot(lo, wrh_ref[...], preferred_element_type=F32)
              + jnp.dot(hi, wrl_ref[...], preferred_element_type=F32)) + br_ref[...]

    tm = logits.shape[0]
    lane = lax.broadcasted_iota(jnp.int32, (1, LANES), 1).astype(F32)
    far = float(LANES)
    is_g = (lane >= GROUP_LANE0) & (lane < GROUP_LANE0 + N_GROUPS)
    gl = jnp.where(is_g, logits, -jnp.inf)
    gmax = jnp.max(gl, -1, keepdims=True)
    gidx = jnp.min(jnp.where(gl == gmax, lane, far), -1, keepdims=True) - GROUP_LANE0
    pg = 1.0 / jnp.sum(jnp.where(is_g, jnp.exp(jnp.where(is_g, logits, gmax) - gmax), 0.0), -1, keepdims=True)
    in_grp = (lane >= gidx * EXPERTS_PER_GROUP) & (lane < (gidx + 1.0) * EXPERTS_PER_GROUP)
    el = jnp.where(in_grp, logits, -jnp.inf)
    v1 = jnp.max(el, -1, keepdims=True)
    i1 = jnp.min(jnp.where(el == v1, lane, far), -1, keepdims=True)
    el2 = jnp.where(lane == i1, -jnp.inf, el)
    v2 = jnp.max(el2, -1, keepdims=True)
    i2 = jnp.min(jnp.where(el2 == v2, lane, far), -1, keepdims=True)
    ex = jnp.exp(v2 - v1)
    w1 = pg / (1.0 + ex)
    w2 = pg * ex / (1.0 + ex)

    first_lo = i1 < i2
    a = jnp.minimum(i1, i2) - gidx * EXPERTS_PER_GROUP
    b = jnp.maximum(i1, i2) - gidx * EXPERTS_PER_GROUP
    pair = a * (2.0 * EXPERTS_PER_GROUP - 1.0 - a) * 0.5 + (b - a - 1.0)
    cls = gidx * PAIRS_PER_GROUP + pair
    w_lo = jnp.where(first_lo, w1, w2)
    w_hi = jnp.where(first_lo, w2, w1)
    rows_ref[:, D_MODEL:] = jnp.where(lane == M_WLO, w_lo, jnp.where(lane == M_WHI, w_hi, 0.0))

    onehot = jnp.where(lane == cls, 1.0, 0.0)
    r_i = lax.broadcasted_iota(jnp.int32, (tm, tm), 0)
    c_i = lax.broadcasted_iota(jnp.int32, (tm, tm), 1)
    lower = jnp.where(c_i < r_i, 1.0, 0.0).astype(BF16)
    prefix = jnp.dot(lower, onehot.astype(BF16), preferred_element_type=F32) + carry_sc[0:1, :]
    rank = _lane_pick(prefix, lane, cls)
    new_carry = carry_sc[0:1, :] + jnp.sum(onehot, 0, keepdims=True)
    carry_sc[...] = jnp.broadcast_to(new_carry, carry_sc.shape)
    counts_ref[...] = jnp.broadcast_to(new_carry, counts_ref.shape)
    route_ref[...] = jnp.where(lane == R_CLASS, cls, jnp.where(lane == R_RANK, rank, 0.0))


def _outproj_router(oa, ob, x2, consts):
    t = x2.shape[0]
    tm = TM_PROJ
    full = lambda a: pl.BlockSpec(a.shape, lambda i: (0,) * a.ndim)
    tile = lambda n: pl.BlockSpec((tm, n), lambda i: (i, 0))
    return pl.pallas_call(
        _outproj_body,
        grid=(t // tm,),
        in_specs=[pl.BlockSpec((A_Q_COLS, tm), lambda i: (0, i)),
                  pl.BlockSpec((B_HEADS * B_V, tm), lambda i: (0, i)), tile(D_MODEL)]
        + [full(a) for a in consts],
        out_specs=[tile(ROW_WORDS), tile(LANES), pl.BlockSpec((8, LANES), lambda i: (0, 0))],
        out_shape=[jax.ShapeDtypeStruct((t, ROW_WORDS), F32),
                   jax.ShapeDtypeStruct((t, LANES), F32), jax.ShapeDtypeStruct((8, LANES), F32)],
        scratch_shapes=[pltpu.VMEM((8, LANES), F32)],
        compiler_params=_cparams(("arbitrary",)),
        name="out_proj_router",
    )(oa, ob, x2, *consts)


T_CLASS, T_USED, T_ELO, T_EHI = range(4)


def _slot_body(n_tiles, route_ref, counts_ref, elo_ref, ehi_ref, slots_ref, tmap_ref):
    lane_i = lax.broadcasted_iota(jnp.int32, (1, LANES), 1)
    lane = lane_i.astype(F32)
    is_c = lane_i < N_CLASSES
    cnt = jnp.where(is_c, counts_ref[0:1, :], 0.0)
    padded = jnp.floor((cnt + (TE - 1)) * (1.0 / TE)) * TE
    incl = padded
    k = 1
    while k < LANES:
        incl = incl + jnp.where(lane_i >= k, pltpu.roll(incl, k, 1), 0.0)
        k *= 2
    off = incl - padded

    route = route_ref[...]
    cls = _lane_pick(route, lane, R_CLASS)
    slot = _lane_pick(jnp.broadcast_to(off, route.shape), lane, cls) + _lane_pick(route, lane, R_RANK)
    slots_ref[...] = jnp.where(lane == 0, slot, 0.0).astype(jnp.int32)

    row0 = lax.broadcasted_iota(jnp.int32, (n_tiles, 1), 0).astype(F32) * TE
    tc = jnp.sum(jnp.where(is_c & (incl <= row0), 1.0, 0.0), -1, keepdims=True)
    tc = jnp.minimum(tc, N_CLASSES - 1.0)
    at_tc = lane == tc
    used = jnp.clip(jnp.sum(jnp.where(at_tc, cnt - (row0 - off), 0.0), -1, keepdims=True), 0.0, float(TE))
    elo = jnp.sum(jnp.where(at_tc, elo_ref[...], 0.0), -1, keepdims=True)
    ehi = jnp.sum(jnp.where(at_tc, ehi_ref[...], 0.0), -1, keepdims=True)
    tmap = jnp.zeros((n_tiles, LANES), F32)
    for idx, val in ((T_CLASS, tc), (T_USED, used), (T_ELO, elo), (T_EHI, ehi)):
        tmap = jnp.where(lane == idx, val, tmap)
    tmap_ref[...] = tmap.astype(jnp.int32)


def _class_experts():
    elo = np.zeros((1, LANES))
    ehi = np.zeros((1, LANES))
    c = 0
    for g in range(N_GROUPS):
        for a in range(EXPERTS_PER_GROUP):
            for b in range(a + 1, EXPERTS_PER_GROUP):
                elo[0, c] = g * EXPERTS_PER_GROUP + a
                ehi[0, c] = g * EXPERTS_PER_GROUP + b
                c += 1
    return jnp.asarray(elo, F32), jnp.asarray(ehi, F32)


def _slot_assign(route, counts, n_tiles):
    t = route.shape[0]
    tm = min(TM_SLOT, t)
    row = pl.BlockSpec((1, LANES), lambda i: (0, 0))
    return pl.pallas_call(
        functools.partial(_slot_body, n_tiles),
        grid=(t // tm,),
        in_specs=[pl.BlockSpec((tm, LANES), lambda i: (i, 0)), pl.BlockSpec((8, LANES), lambda i: (0, 0)), row, row],
        out_specs=[pl.BlockSpec((tm, LANES), lambda i: (i, 0)), pl.BlockSpec((n_tiles, LANES), lambda i: (0, 0))],
        out_shape=[jax.ShapeDtypeStruct((t, LANES), jnp.int32), jax.ShapeDtypeStruct((n_tiles, LANES), jnp.int32)],
        compiler_params=_cparams(("arbitrary",)),
        name="slot_assign",
    )(route, counts, *_class_experts())


def _dispatch_body(tm, slots_ref, src_ref, zeros_ref, dst_ref, sem):
    del zeros_ref

    def row_copy(i, u):
        return pltpu.make_async_copy(src_ref.at[i, pl.ds(u, 1), :],
                                     dst_ref.at[pl.ds(slots_ref[i * ROW_GROUP + u], 1), :], sem)

    _row_dma_loops(tm, row_copy)


def _dispatch(slots, rows, n_rows):
    t = rows.shape[0]
    tm = min(TM_DISP, t)
    return pl.pallas_call(
        functools.partial(_dispatch_body, tm),
        grid=(t // tm,),
        in_specs=[pl.BlockSpec((tm,), lambda i: (i,), memory_space=pltpu.SMEM),
                  pl.BlockSpec((tm // ROW_GROUP, ROW_GROUP, ROW_WORDS), lambda i: (i, 0, 0)),
                  pl.BlockSpec(memory_space=pl.ANY)],
        out_specs=pl.BlockSpec(memory_space=pl.ANY),
        out_shape=jax.ShapeDtypeStruct((n_rows, ROW_WORDS), F32),
        scratch_shapes=[pltpu.SemaphoreType.DMA(())],
        input_output_aliases={2: 0},
        compiler_params=_cparams(("arbitrary",)),
        name="moe_dispatch",
    )(slots, rows.reshape(t // ROW_GROUP, ROW_GROUP, ROW_WORDS), jnp.zeros((n_rows, ROW_WORDS), F32))


def _expert_body(elo_ref, ehi_ref, used_ref, src_ref, rows_ref, wg0_ref, wu0_ref, wd0_ref, wg1_ref, wu1_ref, wd1_ref,
                 g_ref, b_ref, z_ref):
    del elo_ref, ehi_ref, src_ref
    used = used_ref[pl.program_id(0)]

    @pl.when(used > 0)
    def _():
        h1 = rows_ref[:, :D_MODEL]
        meta = rows_ref[:, D_MODEL:]
        lane = lax.broadcasted_iota(jnp.int32, (1, LANES), 1).astype(F32)
        x = h1.astype(BF16)
        ffn = jnp.zeros((TE, D_MODEL), F32)
        for (wg, wu, wd), lane_w in (((wg0_ref, wu0_ref, wd0_ref), M_WLO), ((wg1_ref, wu1_ref, wd1_ref), M_WHI)):
            g = jnp.dot(x, wg[0], preferred_element_type=F32)
            u = jnp.dot(x, wu[0], preferred_element_type=F32)
            hid = (g * (1.0 / (1.0 + jnp.exp(-g))) * u).astype(BF16)
            y = jnp.dot(hid, wd[0], preferred_element_type=F32)
            ffn = ffn + _lane_pick(meta, lane, lane_w) * y
        z_ref[...] = _layer_norm(ALPHA * h1 + ffn, g_ref[...], b_ref[...])

    @pl.when(used <= 0)
    def _():
        z_ref[...] = jnp.zeros(z_ref.shape, F32)


def _expert_mlp(tile_elo, tile_ehi, tile_used, xs, w_gate, w_up, w_down, ln_g, ln_b):
    n_rows = xs.shape[0]
    n_tiles = n_rows // TE
    last_used = jnp.maximum(jnp.sum((tile_used > 0).astype(jnp.int32)) - 1, 0)
    tile_src = jnp.minimum(jnp.arange(n_tiles, dtype=jnp.int32), last_used)
    up_spec = lambda which: pl.BlockSpec((1, D_MODEL, EXPERT_FF),
                                         lambda r, lo, hi, us, src: ((lo, hi)[which][r], 0, 0))
    down_spec = lambda which: pl.BlockSpec((1, EXPERT_FF, D_MODEL),
                                           lambda r, lo, hi, us, src: ((lo, hi)[which][r], 0, 0))
    row = pl.BlockSpec((1, D_MODEL), lambda r, lo, hi, us, src: (0, 0))
    grid_spec = pltpu.PrefetchScalarGridSpec(
        num_scalar_prefetch=4,
        grid=(n_tiles,),
        in_specs=[pl.BlockSpec((TE, ROW_WORDS), lambda r, lo, hi, us, src: (src[r], 0)),
                  up_spec(0), up_spec(0), down_spec(0), up_spec(1), up_spec(1), down_spec(1), row, row],
        out_specs=pl.BlockSpec((TE, D_MODEL), lambda r, lo, hi, us, src: (r, 0)),
    )
    return pl.pallas_call(
        _expert_body,
        grid_spec=grid_spec,
        out_shape=jax.ShapeDtypeStruct((n_rows, D_MODEL), F32),
        compiler_params=_cparams(("arbitrary",)),
        name="expert_mlp",
    )(tile_elo, tile_ehi, tile_used, tile_src, xs, w_gate, w_up, w_down, w_gate, w_up, w_down, ln_g, ln_b)


ROW_GROUP = 8


def _row_dma_loops(tm, row_copy):
    def start(i, carry):
        for u in range(ROW_GROUP):
            row_copy(i, u).start()
        return carry

    def wait(i, carry):
        for u in range(ROW_GROUP):
            row_copy(i, u).wait()
        return carry

    lax.fori_loop(0, tm // ROW_GROUP, start, 0)
    lax.fori_loop(0, tm // ROW_GROUP, wait, 0)


def _undispatch_body(tm, slots_ref, z_ref, o_ref, sem):
    def row_copy(i, u):
        return pltpu.make_async_copy(z_ref.at[pl.ds(slots_ref[i * ROW_GROUP + u], 1), :],
                                     o_ref.at[i, pl.ds(u, 1), :], sem)

    _row_dma_loops(tm, row_copy)


def _undispatch(slots, z):
    t = slots.shape[0]
    tm = min(TM_DISP, t)
    return pl.pallas_call(
        functools.partial(_undispatch_body, tm),
        grid=(t // tm,),
        in_specs=[pl.BlockSpec((tm,), lambda i: (i,), memory_space=pltpu.SMEM),
                  pl.BlockSpec(memory_space=pl.ANY)],
        out_specs=pl.BlockSpec((tm // ROW_GROUP, ROW_GROUP, D_MODEL), lambda i: (i, 0, 0)),
        out_shape=jax.ShapeDtypeStruct((t // ROW_GROUP, ROW_GROUP, D_MODEL), F32),
        scratch_shapes=[pltpu.SemaphoreType.DMA(())],
        compiler_params=_cparams(("arbitrary",)),
        name="moe_undispatch",
    )(slots, z)


def kernel(x, positions, ln_emb_g, ln_emb_b, w_in, a_sink, q_a_norm_g, w_q_b, kv_a_norm_g, w_kv_b, out_norm_a_g, out_norm_b_g, w_out, ln_attn_g, ln_attn_b, w_group, b_group, w_expert, b_expert, w_gate, w_up, w_down, ln_ffn_g, ln_ffn_b):
    batch, seq, d = x.shape
    t = batch * seq
    x2 = x.reshape(t, d)
    row = lambda a: a.reshape(1, -1).astype(F32)

    w1, wva, wq, wk, wv = _proj_weights(w_in[0], w_q_b[0], w_kv_b[0])
    h0, qa, ka, vat, qb, kb, vt = _projections(x2, positions.reshape(t, 1), row(ln_emb_g), row(ln_emb_b),
                                               w1, wva, wq, wk, wv, row(q_a_norm_g[0]), row(kv_a_norm_g[0]))
    oa = _attention_a(qa, ka, vat, a_sink[0].reshape(1, A_HEADS).astype(F32), batch, seq)
    ob = _attention_b(qb, kb, vt, batch, seq)

    wo = w_out[0]
    woa = wo[:A_Q_COLS].astype(BF16)
    wob = wo[A_Q_COLS:].astype(BF16)
    gb = row(out_norm_b_g[0])
    wr = jnp.concatenate([w_expert[0], w_group[0],
                          jnp.zeros((d, LANES - N_EXPERTS - N_GROUPS), F32)], -1)
    wr_hi = wr.astype(BF16)
    wr_lo = (wr - wr_hi.astype(F32)).astype(BF16)
    br = jnp.concatenate([b_expert[0], b_group[0], jnp.zeros((LANES - N_EXPERTS - N_GROUPS,), F32)]).reshape(1, LANES)
    consts = [row(out_norm_a_g[0]), gb, woa, wob, row(ln_attn_g[0]), row(ln_attn_b[0]),
              wr_hi, wr_lo, br]
    rows, route, counts = _outproj_router(oa, ob, h0, consts)

    n_tiles = -(-t // TE) + N_CLASSES
    n_tiles = -(-n_tiles // 8) * 8
    slots, tmap = _slot_assign(route, counts, n_tiles)
    slot = slots[:, 0]
    xs = _dispatch(slot, rows, n_tiles * TE)
    z = _expert_mlp(tmap[:, T_ELO], tmap[:, T_EHI], tmap[:, T_USED], xs,
                    w_gate[0].astype(BF16), w_up[0].astype(BF16), w_down[0].astype(BF16),
                    row(ln_ffn_g[0]), row(ln_ffn_b[0]))
    return _undispatch(slot, z).reshape(batch, seq, d)
```

```python
import functools

import numpy as np
import jax
import jax.numpy as jnp
from jax import lax
from jax.experimental import pallas as pl
from jax.experimental.pallas import tpu as pltpu

F32 = jnp.float32
BF16 = jnp.bfloat16

D_MODEL = 1024
A_HEADS = 8
A_KV_HEADS = 2
A_HEAD_DIM = 64
WINDOW = 128
B_HEADS = 8
B_NOPE = 64
B_ROPE = 32
B_V = 64
B_Q_RANK = 768
B_KV_RANK = 256
ROPE_THETA = 10000.0
N_GROUPS = 4
EXPERTS_PER_GROUP = 8
N_EXPERTS = N_GROUPS * EXPERTS_PER_GROUP
EXPERT_FF = 256
LN_EPS = 1e-5
RMS_EPS = 1e-6
DEPTH = 1
ALPHA = (2 * DEPTH) ** 0.25
NEG_BIG = -1e30
LOG2_E = 1.4426950408889634

LANES = 128
A_Q_COLS = A_HEADS * A_HEAD_DIM
A_KV_COLS = A_KV_HEADS * A_HEAD_DIM
PAD_COLS = B_HEADS * LANES
BF16_SUBLANES = 16
VT_ROWS = B_V + BF16_SUBLANES
A_GROUP = A_HEADS // A_KV_HEADS

C_QA = 0
C_KA = C_QA + A_HEADS * LANES
C_CQ = C_KA + A_KV_COLS
C_CKV = C_CQ + B_Q_RANK
C_KR = C_CKV + B_KV_RANK
W1_COLS = C_KR + LANES

TM_IN_PROJ = 512
TM_PROJ = 256
TQ_A = 256
TQ_B = 1024
TK_B = 256
KT_PER_ITER = 32
TE = 384
TM_SLOT = 1024
TM_DISP = 1024
VMEM_LIMIT = 56 * 1024 * 1024


def _cparams(sem):
    return pltpu.CompilerParams(dimension_semantics=sem, vmem_limit_bytes=VMEM_LIMIT)


def _layer_norm(x, g, b):
    mu = jnp.mean(x, -1, keepdims=True)
    xc = x - mu
    var = jnp.mean(xc * xc, -1, keepdims=True)
    return xc * lax.rsqrt(var + LN_EPS) * g + b


def _rope_block(xc, cos, sin_signed, first_half, half):
    rot = jnp.where(first_half, pltpu.roll(xc, LANES - half, 1), pltpu.roll(xc, half, 1))
    return xc * cos + rot * sin_signed


HALF_A = A_HEAD_DIM // 2
HALF_B = B_ROPE // 2


def _rope_values(pos, inv):
    ang = pos.astype(F32) * inv
    cc = jnp.cos(ang)
    ss = jnp.sin(ang)
    lane = lax.broadcasted_iota(jnp.int32, (1, LANES), 1)

    def table_a(x):
        out = x
        for k in range(1, LANES // HALF_A):
            out = jnp.where(lane >= k * HALF_A, pltpu.roll(x, k * HALF_A, 1), out)
        return out

    def table_b(x, fill):
        lo = pltpu.roll(x, B_NOPE - HALF_A, 1)
        hi = pltpu.roll(x, B_NOPE - HALF_A + HALF_B, 1)
        out = jnp.where(lane < B_NOPE + HALF_B, lo, hi)
        return jnp.where((lane >= B_NOPE) & (lane < B_NOPE + B_ROPE), out, fill)

    sign_a = jnp.where(lane % A_HEAD_DIM < HALF_A, -1.0, 1.0)
    sign_b = jnp.where(lane < B_NOPE + HALF_B, -1.0, 1.0)
    return table_a(cc), table_a(ss) * sign_a, table_b(cc, 1.0), table_b(ss, 0.0) * sign_b


def _rope_rates():
    inv_a = 1.0 / (ROPE_THETA ** (np.arange(0, A_HEAD_DIM, 2, dtype=np.float64) / A_HEAD_DIM))
    inv_b = 1.0 / (ROPE_THETA ** (np.arange(0, B_ROPE, 2, dtype=np.float64) / B_ROPE))
    inv = np.zeros((1, LANES))
    inv[0, :HALF_A] = inv_a
    inv[0, HALF_A:HALF_A + HALF_B] = inv_b
    return jnp.asarray(inv, F32)


def _ones_rows(vt):
    vrow = lax.broadcasted_iota(jnp.int32, (vt.shape[0], 1), 0)
    return jnp.where(vrow % VT_ROWS >= B_V, 1.0, vt)


def _proj_body(x_ref, pos_ref, g_ref, b_ref, w1_ref, wva_ref, wq_ref, wk_ref, wv_ref, gq_ref, gkv_ref, inv_ref,
               h0_ref, qa_ref, ka_ref, vat_ref, qb_ref, kb_ref, vt_ref):
    lane = lax.broadcasted_iota(jnp.int32, (1, LANES), 1)
    first_a = (lane % A_HEAD_DIM) < (A_HEAD_DIM // 2)
    first_b = (lane % B_ROPE) < (B_ROPE // 2)
    scale_a = A_HEAD_DIM ** -0.5 * LOG2_E
    scale_b = (B_NOPE + B_ROPE) ** -0.5 * LOG2_E
    h0 = _layer_norm(x_ref[...], g_ref[...], b_ref[...])
    h0_ref[...] = h0
    h = h0.astype(BF16)
    cosa, sina, cosb, sinb = _rope_values(pos_ref[...], inv_ref[...])

    def proj(c0, width):
        return jnp.dot(h, w1_ref[:, c0:c0 + width], preferred_element_type=F32)

    qa = proj(C_QA, A_HEADS * LANES)
    for c in range(A_HEADS):
        blk = _rope_block(qa[:, c * LANES:(c + 1) * LANES], cosa, sina, first_a, HALF_A)
        qa_ref[:, c * LANES:(c + 1) * LANES] = (blk * scale_a).astype(BF16)
    ka_ref[...] = _rope_block(proj(C_KA, A_KV_COLS), cosa, sina, first_a, HALF_A).astype(BF16)
    vat = lax.dot_general(wva_ref[...], h, (((1,), (1,)), ((), ())), preferred_element_type=F32)
    vat_ref[...] = _ones_rows(vat).astype(BF16)

    cq = proj(C_CQ, B_Q_RANK)
    cq = cq * lax.rsqrt(jnp.mean(cq * cq, -1, keepdims=True) + RMS_EPS) * gq_ref[...]
    qb = jnp.dot(cq.astype(BF16), wq_ref[...], preferred_element_type=F32)
    for c in range(B_HEADS):
        blk = _rope_block(qb[:, c * LANES:(c + 1) * LANES], cosb, sinb, first_b, HALF_B)
        qb_ref[:, c * LANES:(c + 1) * LANES] = (blk * scale_b).astype(BF16)

    ckv = proj(C_CKV, B_KV_RANK)
    ckv = (ckv * lax.rsqrt(jnp.mean(ckv * ckv, -1, keepdims=True) + RMS_EPS) * gkv_ref[...]).astype(BF16)
    kr = _rope_block(proj(C_KR, LANES), cosb, sinb, first_b, HALF_B)
    kn = jnp.dot(ckv, wk_ref[...], preferred_element_type=F32)
    for c in range(B_HEADS):
        kb_ref[:, c * LANES:(c + 1) * LANES] = (kn[:, c * LANES:(c + 1) * LANES] + kr).astype(BF16)
    vt = lax.dot_general(wv_ref[...], ckv, (((1,), (1,)), ((), ())), preferred_element_type=F32)
    vt_ref[...] = _ones_rows(vt).astype(BF16)


def _proj_weights(w_in, w_q_b, w_kv_b):
    w = w_in
    d = w.shape[0]
    o = A_Q_COLS
    ka = w[:, o:o + A_KV_COLS]; o += A_KV_COLS
    va = w[:, o:o + A_KV_COLS]; o += A_KV_COLS
    cq = w[:, o:o + B_Q_RANK]; o += B_Q_RANK
    ckv = w[:, o:o + B_KV_RANK]; o += B_KV_RANK
    kr = w[:, o:o + B_ROPE]
    kr_blk = jnp.concatenate([jnp.zeros((d, B_NOPE), w.dtype), kr,
                              jnp.zeros((d, LANES - B_NOPE - B_ROPE), w.dtype)], -1)
    qa = w[:, :A_Q_COLS].reshape(d, A_KV_HEADS, A_GROUP, A_HEAD_DIM)
    qa = jnp.stack([jnp.pad(qa[:, g], ((0, 0), (0, 0), (g * A_HEAD_DIM, LANES - (g + 1) * A_HEAD_DIM)))
                    for g in range(A_KV_HEADS)], 1).reshape(d, A_HEADS * LANES)
    w1 = jnp.concatenate([qa, ka, cq, ckv, kr_blk], -1).astype(BF16)
    wva = jnp.pad(va.reshape(d, A_KV_HEADS, A_HEAD_DIM), ((0, 0), (0, 0), (0, VT_ROWS - A_HEAD_DIM)))
    wva = wva.reshape(d, A_KV_HEADS * VT_ROWS).T.astype(BF16)
    wq = w_q_b.reshape(B_Q_RANK, B_HEADS, B_NOPE + B_ROPE)
    wq = jnp.pad(wq, ((0, 0), (0, 0), (0, LANES - B_NOPE - B_ROPE))).reshape(B_Q_RANK, PAD_COLS).astype(BF16)
    wkv = w_kv_b.reshape(B_KV_RANK, B_HEADS, B_NOPE + B_V)
    wk = jnp.pad(wkv[:, :, :B_NOPE], ((0, 0), (0, 0), (0, LANES - B_NOPE))).reshape(B_KV_RANK, PAD_COLS).astype(BF16)
    wv = jnp.pad(wkv[:, :, B_NOPE:], ((0, 0), (0, 0), (0, VT_ROWS - B_V)))
    wv = wv.reshape(B_KV_RANK, B_HEADS * VT_ROWS).T.astype(BF16)
    return w1, wva, wq, wk, wv


def _projections(x2, pos, ln_g, ln_b, w1, wva, wq, wk, wv, gq, gkv):
    t = x2.shape[0]
    tm = TM_IN_PROJ
    full = lambda a: pl.BlockSpec(a.shape, lambda i: (0,) * a.ndim)
    tile = lambda n: pl.BlockSpec((tm, n), lambda i: (i, 0))
    consts = [ln_g, ln_b, w1, wva, wq, wk, wv, gq, gkv, _rope_rates()]
    tok = lambda n, dt=BF16: (tile(n), jax.ShapeDtypeStruct((t, n), dt))
    feat = lambda r: (pl.BlockSpec((r, tm), lambda i: (0, i)), jax.ShapeDtypeStruct((r, t), BF16))
    outs = [tok(D_MODEL, F32), tok(A_HEADS * LANES), tok(A_KV_COLS), feat(A_KV_HEADS * VT_ROWS),
            tok(PAD_COLS), tok(PAD_COLS), feat(B_HEADS * VT_ROWS)]
    return pl.pallas_call(
        _proj_body,
        grid=(t // tm,),
        in_specs=[tile(D_MODEL), tile(1)] + [full(a) for a in consts],
        out_specs=[o[0] for o in outs],
        out_shape=[o[1] for o in outs],
        compiler_params=_cparams(("parallel",)),
        name="ln_in_proj",
    )(x2, pos, *consts)


def _attn_a_body(seq, sink_ref, q_ref, k0, k1, k2, k3, v0, v1, v2, v3, o_ref, s_sc):
    i = pl.program_id(1)
    kk = jnp.concatenate([k0[...], k1[...], k2[...], k3[...]], 0)
    vt = jnp.concatenate([v0[...], v1[...], v2[...], v3[...]], 1)
    nk = kk.shape[0]
    kpos = i * TQ_A - WINDOW + lax.broadcasted_iota(jnp.int32, (nk, 1), 0)
    qpos = i * TQ_A + lax.broadcasted_iota(jnp.int32, (1, TQ_A), 1)
    valid = (kpos >= 0) & (kpos < seq) & (jnp.abs(kpos - qpos) <= WINDOW)
    bias = jnp.where(valid, 0.0, NEG_BIG)

    def scores(h):
        qt = q_ref[:, h * LANES:(h + 1) * LANES].astype(F32).T.astype(BF16)
        s = jnp.dot(kk, qt, preferred_element_type=F32) + bias
        return s, jnp.max(s, 0, keepdims=True)

    s0, m_tile = scores(0)
    s_sc[0] = s0
    for h in range(A_HEADS):
        g = h // A_GROUP
        if h + 1 < A_HEADS:
            s_next, mt_next = scores(h + 1)
            s_sc[(h + 1) % 2] = s_next
        sink = sink_ref[0, h] * LOG2_E
        m = jnp.maximum(m_tile, sink)
        p = jnp.exp2(s_sc[h % 2] - m).astype(BF16)
        acc = jnp.dot(vt[g * VT_ROWS:(g + 1) * VT_ROWS, :], p, preferred_element_type=F32)
        den = acc[A_HEAD_DIM:A_HEAD_DIM + 1, :] + jnp.exp2(sink - m)
        o_ref[h * A_HEAD_DIM:(h + 1) * A_HEAD_DIM, :] = acc[:A_HEAD_DIM, :] / den
        if h + 1 < A_HEADS:
            m_tile = mt_next


def _attention_a(qa, ka, vat, sink, batch, seq):
    t = qa.shape[0]
    nq = seq // TQ_A
    nb = seq // WINDOW
    per = TQ_A // WINDOW
    assert per + 2 == 4

    def halo(j):
        return lambda b, i: b * nb + jnp.clip(per * i - 1 + j, 0, nb - 1)

    k_specs = [pl.BlockSpec((WINDOW, A_KV_COLS), lambda b, i, f=halo(j): (f(b, i), 0)) for j in range(per + 2)]
    v_specs = [pl.BlockSpec((A_KV_HEADS * VT_ROWS, WINDOW), lambda b, i, f=halo(j): (0, f(b, i)))
               for j in range(per + 2)]
    return pl.pallas_call(
        functools.partial(_attn_a_body, seq),
        grid=(batch, nq),
        in_specs=[pl.BlockSpec(memory_space=pltpu.SMEM),
                  pl.BlockSpec((TQ_A, A_HEADS * LANES), lambda b, i: (b * nq + i, 0))] + k_specs + v_specs,
        out_specs=pl.BlockSpec((A_Q_COLS, TQ_A), lambda b, i: (0, b * nq + i)),
        out_shape=jax.ShapeDtypeStruct((A_Q_COLS, t), F32),
        scratch_shapes=[pltpu.VMEM((2, TQ_A + 2 * WINDOW, TQ_A), F32)],
        compiler_params=_cparams(("parallel", "parallel")),
        name="window_attn",
    )(sink, qa, *([ka] * (per + 2)), *([vat] * (per + 2)))


def _attn_b_body(nkt, q_ref, k_ref, vt_ref, o_ref, s_sc, acc_sc):
    qt = q_ref[...].astype(F32).T.astype(BF16)
    acc_sc[...] = jnp.zeros(acc_sc.shape, F32)

    def scores(j):
        start = pl.multiple_of(j * TK_B, TK_B)
        s = jnp.dot(k_ref[pl.ds(start, TK_B), :], qt, preferred_element_type=F32)
        return s, jnp.max(s, 0, keepdims=True)

    def update(j, s, m, m_tile):
        start = pl.multiple_of(j * TK_B, TK_B)
        m_new = jnp.maximum(m, m_tile)
        alpha = jnp.exp2(m - m_new)
        p = jnp.exp2(s - m_new).astype(BF16)
        pv = jnp.dot(vt_ref[:, pl.ds(start, TK_B)], p, preferred_element_type=F32)
        acc_sc[...] = alpha * acc_sc[...] + pv
        return m_new

    s0, mt0 = scores(0)
    s_sc[0] = s0

    per_iter = min(KT_PER_ITER, nkt)
    assert nkt % per_iter == 0 and per_iter % 2 == 0

    def group(jj, carry):
        m, m_tile = carry
        for u in range(per_iter):
            j = per_iter * jj + u
            s_next, mt_next = scores(jnp.minimum(j + 1, nkt - 1))
            s_sc[(u + 1) % 2] = s_next
            m = update(j, s_sc[u % 2], m, m_tile)
            m_tile = mt_next
        return m, m_tile

    m0 = jnp.full((1, TQ_B), -jnp.inf, F32)
    lax.fori_loop(0, nkt // per_iter, group, (m0, mt0))
    o_ref[...] = acc_sc[:B_V, :] / acc_sc[B_V:B_V + 1, :]


def _attention_b(qb, kb, vt, batch, seq):
    t = qb.shape[0]
    nq = seq // TQ_B
    nkt = seq // TK_B
    return pl.pallas_call(
        functools.partial(_attn_b_body, nkt),
        grid=(batch * B_HEADS, nq),
        in_specs=[pl.BlockSpec((TQ_B, LANES), lambda bh, i: ((bh // B_HEADS) * nq + i, bh % B_HEADS)),
                  pl.BlockSpec((seq, LANES), lambda bh, i: (bh // B_HEADS, bh % B_HEADS)),
                  pl.BlockSpec((VT_ROWS, seq), lambda bh, i: (bh % B_HEADS, bh // B_HEADS))],
        out_specs=pl.BlockSpec((B_V, TQ_B), lambda bh, i: (bh % B_HEADS, (bh // B_HEADS) * nq + i)),
        out_shape=jax.ShapeDtypeStruct((B_HEADS * B_V, t), F32),
        scratch_shapes=[pltpu.VMEM((2, TK_B, TQ_B), F32), pltpu.VMEM((VT_ROWS, TQ_B), F32)],
        compiler_params=_cparams(("parallel", "arbitrary")),
        name="mla_attn",
    )(qb, kb, vt)


R_CLASS, R_RANK = range(2)
M_WLO, M_WHI = range(2)
GROUP_LANE0 = N_EXPERTS
PAIRS_PER_GROUP = EXPERTS_PER_GROUP * (EXPERTS_PER_GROUP - 1) // 2
N_CLASSES = N_GROUPS * PAIRS_PER_GROUP
ROW_WORDS = D_MODEL + LANES


def _lane_pick(slab, lane, idx):
    return jnp.sum(jnp.where(lane == idx, slab, 0.0), -1, keepdims=True)


def _outproj_body(oa_ref, ob_ref, h0_ref, ga_ref, gb_ref, woa_ref, wob_ref, lg1_ref, lb1_ref,
                  wrh_ref, wrl_ref, br_ref, rows_ref, route_ref, counts_ref, carry_sc):
    @pl.when(pl.program_id(0) == 0)
    def _():
        carry_sc[...] = jnp.zeros(carry_sc.shape, F32)

    oa = oa_ref[...].T
    ob = ob_ref[...].T
    na = oa * lax.rsqrt(jnp.mean(oa * oa, -1, keepdims=True) + RMS_EPS) * ga_ref[...]
    nb = ob * lax.rsqrt(jnp.mean(ob * ob, -1, keepdims=True) + RMS_EPS) * gb_ref[...]
    mixed = (jnp.dot(na.astype(BF16), woa_ref[...], preferred_element_type=F32)
             + jnp.dot(nb.astype(BF16), wob_ref[...], preferred_element_type=F32))
    h1 = _layer_norm(ALPHA * h0_ref[...] + mixed, lg1_ref[...], lb1_ref[...])
    rows_ref[:, :D_MODEL] = h1

    hi = h1.astype(BF16)
    lo = (h1 - hi.astype(F32)).astype(BF16)
    logits = (jnp.dot(hi, wrh_ref[...], preferred_element_type=F32)
              + jnp.dot(lo, wrh_ref[...], preferred_element_type=F32)
              + jnp.dot(hi, wrl_ref[...], preferred_element_type=F32)) + br_ref[...]

    tm = logits.shape[0]
    lane = lax.broadcasted_iota(jnp.int32, (1, LANES), 1).astype(F32)
    far = float(LANES)
    is_g = (lane >= GROUP_LANE0) & (lane < GROUP_LANE0 + N_GROUPS)
    gl = jnp.where(is_g, logits, -jnp.inf)
    gmax = jnp.max(gl, -1, keepdims=True)
    gidx = jnp.min(jnp.where(gl == gmax, lane, far), -1, keepdims=True) - GROUP_LANE0
    pg = 1.0 / jnp.sum(jnp.where(is_g, jnp.exp(jnp.where(is_g, logits, gmax) - gmax), 0.0), -1, keepdims=True)
    in_grp = (lane >= gidx * EXPERTS_PER_GROUP) & (lane < (gidx + 1.0) * EXPERTS_PER_GROUP)
    el = jnp.where(in_grp, logits, -jnp.inf)
    v1 = jnp.max(el, -1, keepdims=True)
    i1 = jnp.min(jnp.where(el == v1, lane, far), -1, keepdims=True)
    el2 = jnp.where(lane == i1, -jnp.inf, el)
    v2 = jnp.max(el2, -1, keepdims=True)
    i2 = jnp.min(jnp.where(el2 == v2, lane, far), -1, keepdims=True)
    ex = jnp.exp(v2 - v1)
    w1 = pg / (1.0 + ex)
    w2 = pg * ex / (1.0 + ex)

    first_lo = i1 < i2
    a = jnp.minimum(i1, i2) - gidx * EXPERTS_PER_GROUP
    b = jnp.maximum(i1, i2) - gidx * EXPERTS_PER_GROUP
    pair = a * (2.0 * EXPERTS_PER_GROUP - 1.0 - a) * 0.5 + (b - a - 1.0)
    cls = gidx * PAIRS_PER_GROUP + pair
    w_lo = jnp.where(first_lo, w1, w2)
    w_hi = jnp.where(first_lo, w2, w1)
    rows_ref[:, D_MODEL:] = jnp.where(lane == M_WLO, w_lo, jnp.where(lane == M_WHI, w_hi, 0.0))

    onehot = jnp.where(lane == cls, 1.0, 0.0)
    r_i = lax.broadcasted_iota(jnp.int32, (tm, tm), 0)
    c_i = lax.broadcasted_iota(jnp.int32, (tm, tm), 1)
    lower = jnp.where(c_i < r_i, 1.0, 0.0).astype(BF16)
    prefix = jnp.dot(lower, onehot.astype(BF16), preferred_element_type=F32) + carry_sc[0:1, :]
    rank = _lane_pick(prefix, lane, cls)
    new_carry = carry_sc[0:1, :] + jnp.sum(onehot, 0, keepdims=True)
    carry_sc[...] = jnp.broadcast_to(new_carry, carry_sc.shape)
    counts_ref[...] = jnp.broadcast_to(new_carry, counts_ref.shape)
    route_ref[...] = jnp.where(lane == R_CLASS, cls, jnp.where(lane == R_RANK, rank, 0.0))


def _outproj_router(oa, ob, x2, consts):
    t = x2.shape[0]
    tm = TM_PROJ
    full = lambda a: pl.BlockSpec(a.shape, lambda i: (0,) * a.ndim)
    tile = lambda n: pl.BlockSpec((tm, n), lambda i: (i, 0))
    return pl.pallas_call(
        _outproj_body,
        grid=(t // tm,),
        in_specs=[pl.BlockSpec((A_Q_COLS, tm), lambda i: (0, i)),
                  pl.BlockSpec((B_HEADS * B_V, tm), lambda i: (0, i)), tile(D_MODEL)]
        + [full(a) for a in consts],
        out_specs=[tile(ROW_WORDS), tile(LANES), pl.BlockSpec((8, LANES), lambda i: (0, 0))],
        out_shape=[jax.ShapeDtypeStruct((t, ROW_WORDS), F32),
                   jax.ShapeDtypeStruct((t, LANES), F32), jax.ShapeDtypeStruct((8, LANES), F32)],
        scratch_shapes=[pltpu.VMEM((8, LANES), F32)],
        compiler_params=_cparams(("arbitrary",)),
        name="out_proj_router",
    )(oa, ob, x2, *consts)


T_CLASS, T_USED, T_ELO, T_EHI = range(4)


def _slot_body(n_tiles, route_ref, counts_ref, elo_ref, ehi_ref, slots_ref, tmap_ref):
    lane_i = lax.broadcasted_iota(jnp.int32, (1, LANES), 1)
    lane = lane_i.astype(F32)
    is_c = lane_i < N_CLASSES
    cnt = jnp.where(is_c, counts_ref[0:1, :], 0.0)
    padded = jnp.floor((cnt + (TE - 1)) * (1.0 / TE)) * TE
    incl = padded
    k = 1
    while k < LANES:
        incl = incl + jnp.where(lane_i >= k, pltpu.roll(incl, k, 1), 0.0)
        k *= 2
    off = incl - padded

    route = route_ref[...]
    cls = _lane_pick(route, lane, R_CLASS)
    slot = _lane_pick(jnp.broadcast_to(off, route.shape), lane, cls) + _lane_pick(route, lane, R_RANK)
    slots_ref[...] = jnp.where(lane == 0, slot, 0.0).astype(jnp.int32)

    row0 = lax.broadcasted_iota(jnp.int32, (n_tiles, 1), 0).astype(F32) * TE
    tc = jnp.sum(jnp.where(is_c & (incl <= row0), 1.0, 0.0), -1, keepdims=True)
    tc = jnp.minimum(tc, N_CLASSES - 1.0)
    at_tc = lane == tc
    used = jnp.clip(jnp.sum(jnp.where(at_tc, cnt - (row0 - off), 0.0), -1, keepdims=True), 0.0, float(TE))
    elo = jnp.sum(jnp.where(at_tc, elo_ref[...], 0.0), -1, keepdims=True)
    ehi = jnp.sum(jnp.where(at_tc, ehi_ref[...], 0.0), -1, keepdims=True)
    tmap = jnp.zeros((n_tiles, LANES), F32)
    for idx, val in ((T_CLASS, tc), (T_USED, used), (T_ELO, elo), (T_EHI, ehi)):
        tmap = jnp.where(lane == idx, val, tmap)
    tmap_ref[...] = tmap.astype(jnp.int32)


def _class_experts():
    elo = np.zeros((1, LANES))
    ehi = np.zeros((1, LANES))
    c = 0
    for g in range(N_GROUPS):
        for a in range(EXPERTS_PER_GROUP):
            for b in range(a + 1, EXPERTS_PER_GROUP):
                elo[0, c] = g * EXPERTS_PER_GROUP + a
                ehi[0, c] = g * EXPERTS_PER_GROUP + b
                c += 1
    return jnp.asarray(elo, F32), jnp.asarray(ehi, F32)


def _slot_assign(route, counts, n_tiles):
    t = route.shape[0]
    tm = min(TM_SLOT, t)
    row = pl.BlockSpec((1, LANES), lambda i: (0, 0))
    return pl.pallas_call(
        functools.partial(_slot_body, n_tiles),
        grid=(t // tm,),
        in_specs=[pl.BlockSpec((tm, LANES), lambda i: (i, 0)), pl.BlockSpec((8, LANES), lambda i: (0, 0)), row, row],
        out_specs=[pl.BlockSpec((tm, LANES), lambda i: (i, 0)), pl.BlockSpec((n_tiles, LANES), lambda i: (0, 0))],
        out_shape=[jax.ShapeDtypeStruct((t, LANES), jnp.int32), jax.ShapeDtypeStruct((n_tiles, LANES), jnp.int32)],
        compiler_params=_cparams(("arbitrary",)),
        name="slot_assign",
    )(route, counts, *_class_experts())


def _dispatch_body(tm, slots_ref, src_ref, zeros_ref, dst_ref, sem):
    del zeros_ref

    def row_copy(i, u):
        return pltpu.make_async_copy(src_ref.at[i, pl.ds(u, 1), :],
                                     dst_ref.at[pl.ds(slots_ref[i * ROW_GROUP + u], 1), :], sem)

    _row_dma_loops(tm, row_copy)


def _dispatch(slots, rows, n_rows):
    t = rows.shape[0]
    tm = min(TM_DISP, t)
    return pl.pallas_call(
        functools.partial(_dispatch_body, tm),
        grid=(t // tm,),
        in_specs=[pl.BlockSpec((tm,), lambda i: (i,), memory_space=pltpu.SMEM),
                  pl.BlockSpec((tm // ROW_GROUP, ROW_GROUP, ROW_WORDS), lambda i: (i, 0, 0)),
                  pl.BlockSpec(memory_space=pl.ANY)],
        out_specs=pl.BlockSpec(memory_space=pl.ANY),
        out_shape=jax.ShapeDtypeStruct((n_rows, ROW_WORDS), F32),
        scratch_shapes=[pltpu.SemaphoreType.DMA(())],
        input_output_aliases={2: 0},
        compiler_params=_cparams(("arbitrary",)),
        name="moe_dispatch",
    )(slots, rows.reshape(t // ROW_GROUP, ROW_GROUP, ROW_WORDS), jnp.zeros((n_rows, ROW_WORDS), F32))


def _expert_body(elo_ref, ehi_ref, used_ref, src_ref, rows_ref, wg0_ref, wu0_ref, wd0_ref, wg1_ref, wu1_ref, wd1_ref,
                 g_ref, b_ref, z_ref):
    del elo_ref, ehi_ref, src_ref
    used = used_ref[pl.program_id(0)]

    @pl.when(used > 0)
    def _():
        h1 = rows_ref[:, :D_MODEL]
        meta = rows_ref[:, D_MODEL:]
        lane = lax.broadcasted_iota(jnp.int32, (1, LANES), 1).astype(F32)
        x = h1.astype(BF16)
        ffn = jnp.zeros((TE, D_MODEL), F32)
        for (wg, wu, wd), lane_w in (((wg0_ref, wu0_ref, wd0_ref), M_WLO), ((wg1_ref, wu1_ref, wd1_ref), M_WHI)):
            g = jnp.dot(x, wg[0], preferred_element_type=F32)
            u = jnp.dot(x, wu[0], preferred_element_type=F32)
            hid = (g * (1.0 / (1.0 + jnp.exp(-g))) * u).astype(BF16)
            y = jnp.dot(hid, wd[0], preferred_element_type=F32)
            ffn = ffn + _lane_pick(meta, lane, lane_w) * y
        z_ref[...] = _layer_norm(ALPHA * h1 + ffn, g_ref[...], b_ref[...])

    @pl.when(used <= 0)
    def _():
        z_ref[...] = jnp.zeros(z_ref.shape, F32)


def _expert_mlp(tile_elo, tile_ehi, tile_used, xs, w_gate, w_up, w_down, ln_g, ln_b):
    n_rows = xs.shape[0]
    n_tiles = n_rows // TE
    last_used = jnp.maximum(jnp.sum((tile_used > 0).astype(jnp.int32)) - 1, 0)
    tile_src = jnp.minimum(jnp.arange(n_tiles, dtype=jnp.int32), last_used)
    up_spec = lambda which: pl.BlockSpec((1, D_MODEL, EXPERT_FF),
                                         lambda r, lo, hi, us, src: ((lo, hi)[which][r], 0, 0))
    down_spec = lambda which: pl.BlockSpec((1, EXPERT_FF, D_MODEL),
                                           lambda r, lo, hi, us, src: ((lo, hi)[which][r], 0, 0))
    row = pl.BlockSpec((1, D_MODEL), lambda r, lo, hi, us, src: (0, 0))
    grid_spec = pltpu.PrefetchScalarGridSpec(
        num_scalar_prefetch=4,
        grid=(n_tiles,),
        in_specs=[pl.BlockSpec((TE, ROW_WORDS), lambda r, lo, hi, us, src: (src[r], 0)),
                  up_spec(0), up_spec(0), down_spec(0), up_spec(1), up_spec(1), down_spec(1), row, row],
        out_specs=pl.BlockSpec((TE, D_MODEL), lambda r, lo, hi, us, src: (r, 0)),
    )
    return pl.pallas_call(
        _expert_body,
        grid_spec=grid_spec,
        out_shape=jax.ShapeDtypeStruct((n_rows, D_MODEL), F32),
        compiler_params=_cparams(("arbitrary",)),
        name="expert_mlp",
    )(tile_elo, tile_ehi, tile_used, tile_src, xs, w_gate, w_up, w_down, w_gate, w_up, w_down, ln_g, ln_b)


ROW_GROUP = 8


def _row_dma_loops(tm, row_copy):
    def start(i, carry):
        for u in range(ROW_GROUP):
            row_copy(i, u).start()
        return carry

    def wait(i, carry):
        for u in range(ROW_GROUP):
            row_copy(i, u).wait()
        return carry

    lax.fori_loop(0, tm // ROW_GROUP, start, 0)
    lax.fori_loop(0, tm // ROW_GROUP, wait, 0)


def _undispatch_body(tm, slots_ref, z_ref, o_ref, sem):
    def row_copy(i, u):
        return pltpu.make_async_copy(z_ref.at[pl.ds(slots_ref[i * ROW_GROUP + u], 1), :],
                                     o_ref.at[i, pl.ds(u, 1), :], sem)

    _row_dma_loops(tm, row_copy)


def _undispatch(slots, z):
    t = slots.shape[0]
    tm = min(TM_DISP, t)
    return pl.pallas_call(
        functools.partial(_undispatch_body, tm),
        grid=(t // tm,),
        in_specs=[pl.BlockSpec((tm,), lambda i: (i,), memory_space=pltpu.SMEM),
                  pl.BlockSpec(memory_space=pl.ANY)],
        out_specs=pl.BlockSpec((tm // ROW_GROUP, ROW_GROUP, D_MODEL), lambda i: (i, 0, 0)),
        out_shape=jax.ShapeDtypeStruct((t // ROW_GROUP, ROW_GROUP, D_MODEL), F32),
        scratch_shapes=[pltpu.SemaphoreType.DMA(())],
        compiler_params=_cparams(("arbitrary",)),
        name="moe_undispatch",
    )(slots, z)


def kernel(x, positions, ln_emb_g, ln_emb_b, w_in, a_sink, q_a_norm_g, w_q_b, kv_a_norm_g, w_kv_b, out_norm_a_g, out_norm_b_g, w_out, ln_attn_g, ln_attn_b, w_group, b_group, w_expert, b_expert, w_gate, w_up, w_down, ln_ffn_g, ln_ffn_b):
    batch, seq, d = x.shape
    t = batch * seq
    x2 = x.reshape(t, d)
    row = lambda a: a.reshape(1, -1).astype(F32)

    w1, wva, wq, wk, wv = _proj_weights(w_in[0], w_q_b[0], w_kv_b[0])
    h0, qa, ka, vat, qb, kb, vt = _projections(x2, positions.reshape(t, 1), row(ln_emb_g), row(ln_emb_b),
                                               w1, wva, wq, wk, wv, row(q_a_norm_g[0]), row(kv_a_norm_g[0]))
    oa = _attention_a(qa, ka, vat, a_sink[0].reshape(1, A_HEADS).astype(F32), batch, seq)
    ob = _attention_b(qb, kb, vt, batch, seq)

    wo = w_out[0]
    woa = wo[:A_Q_COLS].astype(BF16)
    wob = wo[A_Q_COLS:].astype(BF16)
    gb = row(out_norm_b_g[0])
    wr = jnp.concatenate([w_expert[0], w_group[0],
                          jnp.zeros((d, LANES - N_EXPERTS - N_GROUPS), F32)], -1)
    wr_hi = wr.astype(BF16)
    wr_lo = (wr - wr_hi.astype(F32)).astype(BF16)
    br = jnp.concatenate([b_expert[0], b_group[0], jnp.zeros((LANES - N_EXPERTS - N_GROUPS,), F32)]).reshape(1, LANES)
    consts = [row(out_norm_a_g[0]), gb, woa, wob, row(ln_attn_g[0]), row(ln_attn_b[0]),
              wr_hi, wr_lo, br]
    rows, route, counts = _outproj_router(oa, ob, h0, consts)

    n_tiles = -(-t // TE) + N_CLASSES
    n_tiles = -(-n_tiles // 8) * 8
    slots, tmap = _slot_assign(route, counts, n_tiles)
    slot = slots[:, 0]
    xs = _dispatch(slot, rows, n_tiles * TE)
    z = _expert_mlp(tmap[:, T_ELO], tmap[:, T_EHI], tmap[:, T_USED], xs,
                    w_gate[0].astype(BF16), w_up[0].astype(BF16), w_down[0].astype(BF16),
                    row(ln_ffn_g[0]), row(ln_ffn_b[0]))
    return _undispatch(slot, z).reshape(batch, seq, d)
```

```python
import functools

import numpy as np
import jax
import jax.numpy as jnp
from jax import lax
from jax.experimental import pallas as pl
from jax.experimental.pallas import tpu as pltpu

F32 = jnp.float32
BF16 = jnp.bfloat16

D_MODEL = 1024
A_HEADS = 8
A_KV_HEADS = 2
A_HEAD_DIM = 64
WINDOW = 128
B_HEADS = 8
B_NOPE = 64
B_ROPE = 32
B_V = 64
B_Q_RANK = 768
B_KV_RANK = 256
ROPE_THETA = 10000.0
N_GROUPS = 4
EXPERTS_PER_GROUP = 8
N_EXPERTS = N_GROUPS * EXPERTS_PER_GROUP
EXPERT_FF = 256
LN_EPS = 1e-5
RMS_EPS = 1e-6
DEPTH = 1
ALPHA = (2 * DEPTH) ** 0.25
NEG_BIG = -1e30
LOG2_E = 1.4426950408889634

LANES = 128
A_Q_COLS = A_HEADS * A_HEAD_DIM
A_KV_COLS = A_KV_HEADS * A_HEAD_DIM
PAD_COLS = B_HEADS * LANES
BF16_SUBLANES = 16
VT_ROWS = B_V + BF16_SUBLANES
A_GROUP = A_HEADS // A_KV_HEADS

C_QA = 0
C_KA = C_QA + A_HEADS * LANES
C_CQ = C_KA + A_KV_COLS
C_CKV = C_CQ + B_Q_RANK
C_KR = C_CKV + B_KV_RANK
W1_COLS = C_KR + LANES

TM_IN_PROJ = 512
TM_PROJ = 256
TQ_A = 256
TQ_B = 1024
TK_B = 256
KT_PER_ITER = 32
SCORE_BUFFERS = 4
TE = 384
TM_SLOT = 1024
TM_DISP = 1024
VMEM_LIMIT = 56 * 1024 * 1024


def _cparams(sem):
    return pltpu.CompilerParams(dimension_semantics=sem, vmem_limit_bytes=VMEM_LIMIT)


def _layer_norm(x, g, b):
    mu = jnp.mean(x, -1, keepdims=True)
    xc = x - mu
    var = jnp.mean(xc * xc, -1, keepdims=True)
    return xc * lax.rsqrt(var + LN_EPS) * g + b


def _rope_block(xc, cos, sin_signed, first_half, half):
    rot = jnp.where(first_half, pltpu.roll(xc, LANES - half, 1), pltpu.roll(xc, half, 1))
    return xc * cos + rot * sin_signed


HALF_A = A_HEAD_DIM // 2
HALF_B = B_ROPE // 2


def _rope_values(pos, inv):
    ang = pos.astype(F32) * inv
    cc = jnp.cos(ang)
    ss = jnp.sin(ang)
    lane = lax.broadcasted_iota(jnp.int32, (1, LANES), 1)

    def table_a(x):
        out = x
        for k in range(1, LANES // HALF_A):
            out = jnp.where(lane >= k * HALF_A, pltpu.roll(x, k * HALF_A, 1), out)
        return out

    def table_b(x, fill):
        lo = pltpu.roll(x, B_NOPE - HALF_A, 1)
        hi = pltpu.roll(x, B_NOPE - HALF_A + HALF_B, 1)
        out = jnp.where(lane < B_NOPE + HALF_B, lo, hi)
        return jnp.where((lane >= B_NOPE) & (lane < B_NOPE + B_ROPE), out, fill)

    sign_a = jnp.where(lane % A_HEAD_DIM < HALF_A, -1.0, 1.0)
    sign_b = jnp.where(lane < B_NOPE + HALF_B, -1.0, 1.0)
    return table_a(cc), table_a(ss) * sign_a, table_b(cc, 1.0), table_b(ss, 0.0) * sign_b


def _rope_rates():
    inv_a = 1.0 / (ROPE_THETA ** (np.arange(0, A_HEAD_DIM, 2, dtype=np.float64) / A_HEAD_DIM))
    inv_b = 1.0 / (ROPE_THETA ** (np.arange(0, B_ROPE, 2, dtype=np.float64) / B_ROPE))
    inv = np.zeros((1, LANES))
    inv[0, :HALF_A] = inv_a
    inv[0, HALF_A:HALF_A + HALF_B] = inv_b
    return jnp.asarray(inv, F32)


def _ones_rows(vt):
    vrow = lax.broadcasted_iota(jnp.int32, (vt.shape[0], 1), 0)
    return jnp.where(vrow % VT_ROWS >= B_V, 1.0, vt)


def _proj_body(x_ref, pos_ref, g_ref, b_ref, w1_ref, wva_ref, wq_ref, wk_ref, wv_ref, gq_ref, gkv_ref, inv_ref,
               h0_ref, qa_ref, ka_ref, vat_ref, qb_ref, kb_ref, vt_ref):
    lane = lax.broadcasted_iota(jnp.int32, (1, LANES), 1)
    first_a = (lane % A_HEAD_DIM) < (A_HEAD_DIM // 2)
    first_b = (lane % B_ROPE) < (B_ROPE // 2)
    scale_a = A_HEAD_DIM ** -0.5 * LOG2_E
    scale_b = (B_NOPE + B_ROPE) ** -0.5 * LOG2_E
    h0 = _layer_norm(x_ref[...], g_ref[...], b_ref[...])
    h0_ref[...] = h0
    h = h0.astype(BF16)
    cosa, sina, cosb, sinb = _rope_values(pos_ref[...], inv_ref[...])

    def proj(c0, width):
        return jnp.dot(h, w1_ref[:, c0:c0 + width], preferred_element_type=F32)

    qa = proj(C_QA, A_HEADS * LANES)
    for c in range(A_HEADS):
        blk = _rope_block(qa[:, c * LANES:(c + 1) * LANES], cosa, sina, first_a, HALF_A)
        qa_ref[:, c * LANES:(c + 1) * LANES] = (blk * scale_a).astype(BF16)
    ka_ref[...] = _rope_block(proj(C_KA, A_KV_COLS), cosa, sina, first_a, HALF_A).astype(BF16)
    vat = lax.dot_general(wva_ref[...], h, (((1,), (1,)), ((), ())), preferred_element_type=F32)
    vat_ref[...] = _ones_rows(vat).astype(BF16)

    cq = proj(C_CQ, B_Q_RANK)
    cq = cq * lax.rsqrt(jnp.mean(cq * cq, -1, keepdims=True) + RMS_EPS) * gq_ref[...]
    qb = jnp.dot(cq.astype(BF16), wq_ref[...], preferred_element_type=F32)
    for c in range(B_HEADS):
        blk = _rope_block(qb[:, c * LANES:(c + 1) * LANES], cosb, sinb, first_b, HALF_B)
        qb_ref[:, c * LANES:(c + 1) * LANES] = (blk * scale_b).astype(BF16)

    ckv = proj(C_CKV, B_KV_RANK)
    ckv = (ckv * lax.rsqrt(jnp.mean(ckv * ckv, -1, keepdims=True) + RMS_EPS) * gkv_ref[...]).astype(BF16)
    kr = _rope_block(proj(C_KR, LANES), cosb, sinb, first_b, HALF_B)
    kn = jnp.dot(ckv, wk_ref[...], preferred_element_type=F32)
    for c in range(B_HEADS):
        kb_ref[:, c * LANES:(c + 1) * LANES] = (kn[:, c * LANES:(c + 1) * LANES] + kr).astype(BF16)
    vt = lax.dot_general(wv_ref[...], ckv, (((1,), (1,)), ((), ())), preferred_element_type=F32)
    vt_ref[...] = _ones_rows(vt).astype(BF16)


def _proj_weights(w_in, w_q_b, w_kv_b):
    w = w_in
    d = w.shape[0]
    o = A_Q_COLS
    ka = w[:, o:o + A_KV_COLS]; o += A_KV_COLS
    va = w[:, o:o + A_KV_COLS]; o += A_KV_COLS
    cq = w[:, o:o + B_Q_RANK]; o += B_Q_RANK
    ckv = w[:, o:o + B_KV_RANK]; o += B_KV_RANK
    kr = w[:, o:o + B_ROPE]
    kr_blk = jnp.concatenate([jnp.zeros((d, B_NOPE), w.dtype), kr,
                              jnp.zeros((d, LANES - B_NOPE - B_ROPE), w.dtype)], -1)
    qa = w[:, :A_Q_COLS].reshape(d, A_KV_HEADS, A_GROUP, A_HEAD_DIM)
    qa = jnp.stack([jnp.pad(qa[:, g], ((0, 0), (0, 0), (g * A_HEAD_DIM, LANES - (g + 1) * A_HEAD_DIM)))
                    for g in range(A_KV_HEADS)], 1).reshape(d, A_HEADS * LANES)
    w1 = jnp.concatenate([qa, ka, cq, ckv, kr_blk], -1).astype(BF16)
    wva = jnp.pad(va.reshape(d, A_KV_HEADS, A_HEAD_DIM), ((0, 0), (0, 0), (0, VT_ROWS - A_HEAD_DIM)))
    wva = wva.reshape(d, A_KV_HEADS * VT_ROWS).T.astype(BF16)
    wq = w_q_b.reshape(B_Q_RANK, B_HEADS, B_NOPE + B_ROPE)
    wq = jnp.pad(wq, ((0, 0), (0, 0), (0, LANES - B_NOPE - B_ROPE))).reshape(B_Q_RANK, PAD_COLS).astype(BF16)
    wkv = w_kv_b.reshape(B_KV_RANK, B_HEADS, B_NOPE + B_V)
    wk = jnp.pad(wkv[:, :, :B_NOPE], ((0, 0), (0, 0), (0, LANES - B_NOPE))).reshape(B_KV_RANK, PAD_COLS).astype(BF16)
    wv = jnp.pad(wkv[:, :, B_NOPE:], ((0, 0), (0, 0), (0, VT_ROWS - B_V)))
    wv = wv.reshape(B_KV_RANK, B_HEADS * VT_ROWS).T.astype(BF16)
    return w1, wva, wq, wk, wv


def _projections(x2, pos, ln_g, ln_b, w1, wva, wq, wk, wv, gq, gkv):
    t = x2.shape[0]
    tm = TM_IN_PROJ
    full = lambda a: pl.BlockSpec(a.shape, lambda i: (0,) * a.ndim)
    tile = lambda n: pl.BlockSpec((tm, n), lambda i: (i, 0))
    consts = [ln_g, ln_b, w1, wva, wq, wk, wv, gq, gkv, _rope_rates()]
    tok = lambda n, dt=BF16: (tile(n), jax.ShapeDtypeStruct((t, n), dt))
    feat = lambda r: (pl.BlockSpec((r, tm), lambda i: (0, i)), jax.ShapeDtypeStruct((r, t), BF16))
    outs = [tok(D_MODEL, F32), tok(A_HEADS * LANES), tok(A_KV_COLS), feat(A_KV_HEADS * VT_ROWS),
            tok(PAD_COLS), tok(PAD_COLS), feat(B_HEADS * VT_ROWS)]
    return pl.pallas_call(
        _proj_body,
        grid=(t // tm,),
        in_specs=[tile(D_MODEL), tile(1)] + [full(a) for a in consts],
        out_specs=[o[0] for o in outs],
        out_shape=[o[1] for o in outs],
        compiler_params=_cparams(("parallel",)),
        name="ln_in_proj",
    )(x2, pos, *consts)


def _attn_a_body(seq, sink_ref, q_ref, k0, k1, k2, k3, v0, v1, v2, v3, o_ref, s_sc):
    i = pl.program_id(1)
    kk = jnp.concatenate([k0[...], k1[...], k2[...], k3[...]], 0)
    vt = jnp.concatenate([v0[...], v1[...], v2[...], v3[...]], 1)
    nk = kk.shape[0]
    kpos = i * TQ_A - WINDOW + lax.broadcasted_iota(jnp.int32, (nk, 1), 0)
    qpos = i * TQ_A + lax.broadcasted_iota(jnp.int32, (1, TQ_A), 1)
    valid = (kpos >= 0) & (kpos < seq) & (jnp.abs(kpos - qpos) <= WINDOW)
    bias = jnp.where(valid, 0.0, NEG_BIG)

    def scores(h):
        qt = q_ref[:, h * LANES:(h + 1) * LANES].astype(F32).T.astype(BF16)
        s = jnp.dot(kk, qt, preferred_element_type=F32) + bias
        return s, jnp.max(s, 0, keepdims=True)

    s0, m_tile = scores(0)
    s_sc[0] = s0
    for h in range(A_HEADS):
        g = h // A_GROUP
        if h + 1 < A_HEADS:
            s_next, mt_next = scores(h + 1)
            s_sc[(h + 1) % 2] = s_next
        sink = sink_ref[0, h] * LOG2_E
        m = jnp.maximum(m_tile, sink)
        p = jnp.exp2(s_sc[h % 2] - m).astype(BF16)
        acc = jnp.dot(vt[g * VT_ROWS:(g + 1) * VT_ROWS, :], p, preferred_element_type=F32)
        den = acc[A_HEAD_DIM:A_HEAD_DIM + 1, :] + jnp.exp2(sink - m)
        o_ref[h * A_HEAD_DIM:(h + 1) * A_HEAD_DIM, :] = acc[:A_HEAD_DIM, :] / den
        if h + 1 < A_HEADS:
            m_tile = mt_next


def _attention_a(qa, ka, vat, sink, batch, seq):
    t = qa.shape[0]
    nq = seq // TQ_A
    nb = seq // WINDOW
    per = TQ_A // WINDOW
    assert per + 2 == 4

    def halo(j):
        return lambda b, i: b * nb + jnp.clip(per * i - 1 + j, 0, nb - 1)

    k_specs = [pl.BlockSpec((WINDOW, A_KV_COLS), lambda b, i, f=halo(j): (f(b, i), 0)) for j in range(per + 2)]
    v_specs = [pl.BlockSpec((A_KV_HEADS * VT_ROWS, WINDOW), lambda b, i, f=halo(j): (0, f(b, i)))
               for j in range(per + 2)]
    return pl.pallas_call(
        functools.partial(_attn_a_body, seq),
        grid=(batch, nq),
        in_specs=[pl.BlockSpec(memory_space=pltpu.SMEM),
                  pl.BlockSpec((TQ_A, A_HEADS * LANES), lambda b, i: (b * nq + i, 0))] + k_specs + v_specs,
        out_specs=pl.BlockSpec((A_Q_COLS, TQ_A), lambda b, i: (0, b * nq + i)),
        out_shape=jax.ShapeDtypeStruct((A_Q_COLS, t), F32),
        scratch_shapes=[pltpu.VMEM((2, TQ_A + 2 * WINDOW, TQ_A), F32)],
        compiler_params=_cparams(("parallel", "parallel")),
        name="window_attn",
    )(sink, qa, *([ka] * (per + 2)), *([vat] * (per + 2)))


def _attn_b_body(nkt, q_ref, k_ref, vt_ref, o_ref, s_sc, acc_sc):
    qt = q_ref[...].astype(F32).T.astype(BF16)
    acc_sc[...] = jnp.zeros(acc_sc.shape, F32)

    def scores(j):
        start = pl.multiple_of(j * TK_B, TK_B)
        s = jnp.dot(k_ref[pl.ds(start, TK_B), :], qt, preferred_element_type=F32)
        return s, jnp.max(s, 0, keepdims=True)

    def update(j, s, m, m_tile):
        start = pl.multiple_of(j * TK_B, TK_B)
        m_new = jnp.maximum(m, m_tile)
        alpha = jnp.exp2(m - m_new)
        p = jnp.exp2(s - m_new).astype(BF16)
        pv = jnp.dot(vt_ref[:, pl.ds(start, TK_B)], p, preferred_element_type=F32)
        acc_sc[...] = alpha * acc_sc[...] + pv
        return m_new

    s0, mt0 = scores(0)
    s_sc[0] = s0
    s1, mt1 = scores(1)
    s_sc[1] = s1

    per_iter = min(KT_PER_ITER, nkt)
    assert nkt % per_iter == 0 and per_iter % SCORE_BUFFERS == 0

    def group(jj, carry):
        m, mt_a, mt_b = carry
        for u in range(per_iter):
            j = per_iter * jj + u
            s_next, mt_next = scores(jnp.minimum(j + 2, nkt - 1))
            s_sc[(u + 2) % SCORE_BUFFERS] = s_next
            m = update(j, s_sc[u % SCORE_BUFFERS], m, mt_a)
            mt_a, mt_b = mt_b, mt_next
        return m, mt_a, mt_b

    m0 = jnp.full((1, TQ_B), -jnp.inf, F32)
    lax.fori_loop(0, nkt // per_iter, group, (m0, mt0, mt1))
    o_ref[...] = acc_sc[:B_V, :] / acc_sc[B_V:B_V + 1, :]


def _attention_b(qb, kb, vt, batch, seq):
    t = qb.shape[0]
    nq = seq // TQ_B
    nkt = seq // TK_B
    return pl.pallas_call(
        functools.partial(_attn_b_body, nkt),
        grid=(batch * B_HEADS, nq),
        in_specs=[pl.BlockSpec((TQ_B, LANES), lambda bh, i: ((bh // B_HEADS) * nq + i, bh % B_HEADS)),
                  pl.BlockSpec((seq, LANES), lambda bh, i: (bh // B_HEADS, bh % B_HEADS)),
                  pl.BlockSpec((VT_ROWS, seq), lambda bh, i: (bh % B_HEADS, bh // B_HEADS))],
        out_specs=pl.BlockSpec((B_V, TQ_B), lambda bh, i: (bh % B_HEADS, (bh // B_HEADS) * nq + i)),
        out_shape=jax.ShapeDtypeStruct((B_HEADS * B_V, t), F32),
        scratch_shapes=[pltpu.VMEM((SCORE_BUFFERS, TK_B, TQ_B), F32), pltpu.VMEM((VT_ROWS, TQ_B), F32)],
        compiler_params=_cparams(("parallel", "arbitrary")),
        name="mla_attn",
    )(qb, kb, vt)


R_CLASS, R_RANK = range(2)
M_WLO, M_WHI = range(2)
GROUP_LANE0 = N_EXPERTS
PAIRS_PER_GROUP = EXPERTS_PER_GROUP * (EXPERTS_PER_GROUP - 1) // 2
N_CLASSES = N_GROUPS * PAIRS_PER_GROUP
ROW_WORDS = D_MODEL + LANES


def _lane_pick(slab, lane, idx):
    return jnp.sum(jnp.where(lane == idx, slab, 0.0), -1, keepdims=True)


def _outproj_body(oa_ref, ob_ref, h0_ref, ga_ref, gb_ref, woa_ref, wob_ref, lg1_ref, lb1_ref,
                  wrh_ref, wrl_ref, br_ref, rows_ref, route_ref, counts_ref, carry_sc):
    @pl.when(pl.program_id(0) == 0)
    def _():
        carry_sc[...] = jnp.zeros(carry_sc.shape, F32)

    oa = oa_ref[...].T
    ob = ob_ref[...].T
    na = oa * lax.rsqrt(jnp.mean(oa * oa, -1, keepdims=True) + RMS_EPS) * ga_ref[...]
    nb = ob * lax.rsqrt(jnp.mean(ob * ob, -1, keepdims=True) + RMS_EPS) * gb_ref[...]
    mixed = (jnp.dot(na.astype(BF16), woa_ref[...], preferred_element_type=F32)
             + jnp.dot(nb.astype(BF16), wob_ref[...], preferred_element_type=F32))
    h1 = _layer_norm(ALPHA * h0_ref[...] + mixed, lg1_ref[...], lb1_ref[...])
    rows_ref[:, :D_MODEL] = h1

    hi = h1.astype(BF16)
    lo = (h1 - hi.astype(F32)).astype(BF16)
    logits = (jnp.dot(hi, wrh_ref[...], preferred_element_type=F32)
              + jnp.dot(lo, wrh_ref[...], preferred_element_type=F32)
              + jnp.dot(hi, wrl_ref[...], preferred_element_type=F32)) + br_ref[...]

    tm = logits.shape[0]
    lane = lax.broadcasted_iota(jnp.int32, (1, LANES), 1).astype(F32)
    far = float(LANES)
    is_g = (lane >= GROUP_LANE0) & (lane < GROUP_LANE0 + N_GROUPS)
    gl = jnp.where(is_g, logits, -jnp.inf)
    gmax = jnp.max(gl, -1, keepdims=True)
    gidx = jnp.min(jnp.where(gl == gmax, lane, far), -1, keepdims=True) - GROUP_LANE0
    pg = 1.0 / jnp.sum(jnp.where(is_g, jnp.exp(jnp.where(is_g, logits, gmax) - gmax), 0.0), -1, keepdims=True)
    in_grp = (lane >= gidx * EXPERTS_PER_GROUP) & (lane < (gidx + 1.0) * EXPERTS_PER_GROUP)
    el = jnp.where(in_grp, logits, -jnp.inf)
    v1 = jnp.max(el, -1, keepdims=True)
    i1 = jnp.min(jnp.where(el == v1, lane, far), -1, keepdims=True)
    el2 = jnp.where(lane == i1, -jnp.inf, el)
    v2 = jnp.max(el2, -1, keepdims=True)
    i2 = jnp.min(jnp.where(el2 == v2, lane, far), -1, keepdims=True)
    ex = jnp.exp(v2 - v1)
    w1 = pg / (1.0 + ex)
    w2 = pg * ex / (1.0 + ex)

    first_lo = i1 < i2
    a = jnp.minimum(i1, i2) - gidx * EXPERTS_PER_GROUP
    b = jnp.maximum(i1, i2) - gidx * EXPERTS_PER_GROUP
    pair = a * (2.0 * EXPERTS_PER_GROUP - 1.0 - a) * 0.5 + (b - a - 1.0)
    cls = gidx * PAIRS_PER_GROUP + pair
    w_lo = jnp.where(first_lo, w1, w2)
    w_hi = jnp.where(first_lo, w2, w1)
    rows_ref[:, D_MODEL:] = jnp.where(lane == M_WLO, w_lo, jnp.where(lane == M_WHI, w_hi, 0.0))

    onehot = jnp.where(lane == cls, 1.0, 0.0)
    r_i = lax.broadcasted_iota(jnp.int32, (tm, tm), 0)
    c_i = lax.broadcasted_iota(jnp.int32, (tm, tm), 1)
    lower = jnp.where(c_i < r_i, 1.0, 0.0).astype(BF16)
    prefix = jnp.dot(lower, onehot.astype(BF16), preferred_element_type=F32) + carry_sc[0:1, :]
    rank = _lane_pick(prefix, lane, cls)
    new_carry = carry_sc[0:1, :] + jnp.sum(onehot, 0, keepdims=True)
    carry_sc[...] = jnp.broadcast_to(new_carry, carry_sc.shape)
    counts_ref[...] = jnp.broadcast_to(new_carry, counts_ref.shape)
    route_ref[...] = jnp.where(lane == R_CLASS, cls, jnp.where(lane == R_RANK, rank, 0.0))


def _outproj_router(oa, ob, x2, consts):
    t = x2.shape[0]
    tm = TM_PROJ
    full = lambda a: pl.BlockSpec(a.shape, lambda i: (0,) * a.ndim)
    tile = lambda n: pl.BlockSpec((tm, n), lambda i: (i, 0))
    return pl.pallas_call(
        _outproj_body,
        grid=(t // tm,),
        in_specs=[pl.BlockSpec((A_Q_COLS, tm), lambda i: (0, i)),
                  pl.BlockSpec((B_HEADS * B_V, tm), lambda i: (0, i)), tile(D_MODEL)]
        + [full(a) for a in consts],
        out_specs=[tile(ROW_WORDS), tile(LANES), pl.BlockSpec((8, LANES), lambda i: (0, 0))],
        out_shape=[jax.ShapeDtypeStruct((t, ROW_WORDS), F32),
                   jax.ShapeDtypeStruct((t, LANES), F32), jax.ShapeDtypeStruct((8, LANES), F32)],
        scratch_shapes=[pltpu.VMEM((8, LANES), F32)],
        compiler_params=_cparams(("arbitrary",)),
        name="out_proj_router",
    )(oa, ob, x2, *consts)


T_CLASS, T_USED, T_ELO, T_EHI = range(4)


def _slot_body(n_tiles, route_ref, counts_ref, elo_ref, ehi_ref, slots_ref, tmap_ref):
    lane_i = lax.broadcasted_iota(jnp.int32, (1, LANES), 1)
    lane = lane_i.astype(F32)
    is_c = lane_i < N_CLASSES
    cnt = jnp.where(is_c, counts_ref[0:1, :], 0.0)
    padded = jnp.floor((cnt + (TE - 1)) * (1.0 / TE)) * TE
    incl = padded
    k = 1
    while k < LANES:
        incl = incl + jnp.where(lane_i >= k, pltpu.roll(incl, k, 1), 0.0)
        k *= 2
    off = incl - padded

    route = route_ref[...]
    cls = _lane_pick(route, lane, R_CLASS)
    slot = _lane_pick(jnp.broadcast_to(off, route.shape), lane, cls) + _lane_pick(route, lane, R_RANK)
    slots_ref[...] = jnp.where(lane == 0, slot, 0.0).astype(jnp.int32)

    row0 = lax.broadcasted_iota(jnp.int32, (n_tiles, 1), 0).astype(F32) * TE
    tc = jnp.sum(jnp.where(is_c & (incl <= row0), 1.0, 0.0), -1, keepdims=True)
    tc = jnp.minimum(tc, N_CLASSES - 1.0)
    at_tc = lane == tc
    used = jnp.clip(jnp.sum(jnp.where(at_tc, cnt - (row0 - off), 0.0), -1, keepdims=True), 0.0, float(TE))
    elo = jnp.sum(jnp.where(at_tc, elo_ref[...], 0.0), -1, keepdims=True)
    ehi = jnp.sum(jnp.where(at_tc, ehi_ref[...], 0.0), -1, keepdims=True)
    tmap = jnp.zeros((n_tiles, LANES), F32)
    for idx, val in ((T_CLASS, tc), (T_USED, used), (T_ELO, elo), (T_EHI, ehi)):
        tmap = jnp.where(lane == idx, val, tmap)
    tmap_ref[...] = tmap.astype(jnp.int32)


def _class_experts():
    elo = np.zeros((1, LANES))
    ehi = np.zeros((1, LANES))
    c = 0
    for g in range(N_GROUPS):
        for a in range(EXPERTS_PER_GROUP):
            for b in range(a + 1, EXPERTS_PER_GROUP):
                elo[0, c] = g * EXPERTS_PER_GROUP + a
                ehi[0, c] = g * EXPERTS_PER_GROUP + b
                c += 1
    return jnp.asarray(elo, F32), jnp.asarray(ehi, F32)


def _slot_assign(route, counts, n_tiles):
    t = route.shape[0]
    tm = min(TM_SLOT, t)
    row = pl.BlockSpec((1, LANES), lambda i: (0, 0))
    return pl.pallas_call(
        functools.partial(_slot_body, n_tiles),
        grid=(t // tm,),
        in_specs=[pl.BlockSpec((tm, LANES), lambda i: (i, 0)), pl.BlockSpec((8, LANES), lambda i: (0, 0)), row, row],
        out_specs=[pl.BlockSpec((tm, LANES), lambda i: (i, 0)), pl.BlockSpec((n_tiles, LANES), lambda i: (0, 0))],
        out_shape=[jax.ShapeDtypeStruct((t, LANES), jnp.int32), jax.ShapeDtypeStruct((n_tiles, LANES), jnp.int32)],
        compiler_params=_cparams(("arbitrary",)),
        name="slot_assign",
    )(route, counts, *_class_experts())


def _dispatch_body(tm, slots_ref, src_ref, zeros_ref, dst_ref, sem):
    del zeros_ref

    def row_copy(i, u):
        return pltpu.make_async_copy(src_ref.at[i, pl.ds(u, 1), :],
                                     dst_ref.at[pl.ds(slots_ref[i * ROW_GROUP + u], 1), :], sem)

    _row_dma_loops(tm, row_copy)


def _dispatch(slots, rows, n_rows):
    t = rows.shape[0]
    tm = min(TM_DISP, t)
    return pl.pallas_call(
        functools.partial(_dispatch_body, tm),
        grid=(t // tm,),
        in_specs=[pl.BlockSpec((tm,), lambda i: (i,), memory_space=pltpu.SMEM),
                  pl.BlockSpec((tm // ROW_GROUP, ROW_GROUP, ROW_WORDS), lambda i: (i, 0, 0)),
                  pl.BlockSpec(memory_space=pl.ANY)],
        out_specs=pl.BlockSpec(memory_space=pl.ANY),
        out_shape=jax.ShapeDtypeStruct((n_rows, ROW_WORDS), F32),
        scratch_shapes=[pltpu.SemaphoreType.DMA(())],
        input_output_aliases={2: 0},
        compiler_params=_cparams(("arbitrary",)),
        name="moe_dispatch",
    )(slots, rows.reshape(t // ROW_GROUP, ROW_GROUP, ROW_WORDS), jnp.zeros((n_rows, ROW_WORDS), F32))


def _expert_body(elo_ref, ehi_ref, used_ref, src_ref, rows_ref, wg0_ref, wu0_ref, wd0_ref, wg1_ref, wu1_ref, wd1_ref,
                 g_ref, b_ref, z_ref):
    del elo_ref, ehi_ref, src_ref
    used = used_ref[pl.program_id(0)]

    @pl.when(used > 0)
    def _():
        h1 = rows_ref[:, :D_MODEL]
        meta = rows_ref[:, D_MODEL:]
        lane = lax.broadcasted_iota(jnp.int32, (1, LANES), 1).astype(F32)
        x = h1.astype(BF16)
        ffn = jnp.zeros((TE, D_MODEL), F32)
        for (wg, wu, wd), lane_w in (((wg0_ref, wu0_ref, wd0_ref), M_WLO), ((wg1_ref, wu1_ref, wd1_ref), M_WHI)):
            g = jnp.dot(x, wg[0], preferred_element_type=F32)
            u = jnp.dot(x, wu[0], preferred_element_type=F32)
            hid = (g * (1.0 / (1.0 + jnp.exp(-g))) * u).astype(BF16)
            y = jnp.dot(hid, wd[0], preferred_element_type=F32)
            ffn = ffn + _lane_pick(meta, lane, lane_w) * y
        z_ref[...] = _layer_norm(ALPHA * h1 + ffn, g_ref[...], b_ref[...])

    @pl.when(used <= 0)
    def _():
        z_ref[...] = jnp.zeros(z_ref.shape, F32)


def _expert_mlp(tile_elo, tile_ehi, tile_used, xs, w_gate, w_up, w_down, ln_g, ln_b):
    n_rows = xs.shape[0]
    n_tiles = n_rows // TE
    last_used = jnp.maximum(jnp.sum((tile_used > 0).astype(jnp.int32)) - 1, 0)
    tile_src = jnp.minimum(jnp.arange(n_tiles, dtype=jnp.int32), last_used)
    up_spec = lambda which: pl.BlockSpec((1, D_MODEL, EXPERT_FF),
                                         lambda r, lo, hi, us, src: ((lo, hi)[which][r], 0, 0))
    down_spec = lambda which: pl.BlockSpec((1, EXPERT_FF, D_MODEL),
                                           lambda r, lo, hi, us, src: ((lo, hi)[which][r], 0, 0))
    row = pl.BlockSpec((1, D_MODEL), lambda r, lo, hi, us, src: (0, 0))
    grid_spec = pltpu.PrefetchScalarGridSpec(
        num_scalar_prefetch=4,
        grid=(n_tiles,),
        in_specs=[pl.BlockSpec((TE, ROW_WORDS), lambda r, lo, hi, us, src: (src[r], 0)),
                  up_spec(0), up_spec(0), down_spec(0), up_spec(1), up_spec(1), down_spec(1), row, row],
        out_specs=pl.BlockSpec((TE, D_MODEL), lambda r, lo, hi, us, src: (r, 0)),
    )
    return pl.pallas_call(
        _expert_body,
        grid_spec=grid_spec,
        out_shape=jax.ShapeDtypeStruct((n_rows, D_MODEL), F32),
        compiler_params=_cparams(("arbitrary",)),
        name="expert_mlp",
    )(tile_elo, tile_ehi, tile_used, tile_src, xs, w_gate, w_up, w_down, w_gate, w_up, w_down, ln_g, ln_b)


ROW_GROUP = 8


def _row_dma_loops(tm, row_copy):
    def start(i, carry):
        for u in range(ROW_GROUP):
            row_copy(i, u).start()
        return carry

    def wait(i, carry):
        for u in range(ROW_GROUP):
            row_copy(i, u).wait()
        return carry

    lax.fori_loop(0, tm // ROW_GROUP, start, 0)
    lax.fori_loop(0, tm // ROW_GROUP, wait, 0)


def _undispatch_body(tm, slots_ref, z_ref, o_ref, sem):
    def row_copy(i, u):
        return pltpu.make_async_copy(z_ref.at[pl.ds(slots_ref[i * ROW_GROUP + u], 1), :],
                                     o_ref.at[i, pl.ds(u, 1), :], sem)

    _row_dma_loops(tm, row_copy)


def _undispatch(slots, z):
    t = slots.shape[0]
    tm = min(TM_DISP, t)
    return pl.pallas_call(
        functools.partial(_undispatch_body, tm),
        grid=(t // tm,),
        in_specs=[pl.BlockSpec((tm,), lambda i: (i,), memory_space=pltpu.SMEM),
                  pl.BlockSpec(memory_space=pl.ANY)],
        out_specs=pl.BlockSpec((tm // ROW_GROUP, ROW_GROUP, D_MODEL), lambda i: (i, 0, 0)),
        out_shape=jax.ShapeDtypeStruct((t // ROW_GROUP, ROW_GROUP, D_MODEL), F32),
        scratch_shapes=[pltpu.SemaphoreType.DMA(())],
        compiler_params=_cparams(("arbitrary",)),
        name="moe_undispatch",
    )(slots, z)


def kernel(x, positions, ln_emb_g, ln_emb_b, w_in, a_sink, q_a_norm_g, w_q_b, kv_a_norm_g, w_kv_b, out_norm_a_g, out_norm_b_g, w_out, ln_attn_g, ln_attn_b, w_group, b_group, w_expert, b_expert, w_gate, w_up, w_down, ln_ffn_g, ln_ffn_b):
    batch, seq, d = x.shape
    t = batch * seq
    x2 = x.reshape(t, d)
    row = lambda a: a.reshape(1, -1).astype(F32)

    w1, wva, wq, wk, wv = _proj_weights(w_in[0], w_q_b[0], w_kv_b[0])
    h0, qa, ka, vat, qb, kb, vt = _projections(x2, positions.reshape(t, 1), row(ln_emb_g), row(ln_emb_b),
                                               w1, wva, wq, wk, wv, row(q_a_norm_g[0]), row(kv_a_norm_g[0]))
    oa = _attention_a(qa, ka, vat, a_sink[0].reshape(1, A_HEADS).astype(F32), batch, seq)
    ob = _attention_b(qb, kb, vt, batch, seq)

    wo = w_out[0]
    woa = wo[:A_Q_COLS].astype(BF16)
    wob = wo[A_Q_COLS:].astype(BF16)
    gb = row(out_norm_b_g[0])
    wr = jnp.concatenate([w_expert[0], w_group[0],
                          jnp.zeros((d, LANES - N_EXPERTS - N_GROUPS), F32)], -1)
    wr_hi = wr.astype(BF16)
    wr_lo = (wr - wr_hi.astype(F32)).astype(BF16)
    br = jnp.concatenate([b_expert[0], b_group[0], jnp.zeros((LANES - N_EXPERTS - N_GROUPS,), F32)]).reshape(1, LANES)
    consts = [row(out_norm_a_g[0]), gb, woa, wob, row(ln_attn_g[0]), row(ln_attn_b[0]),
              wr_hi, wr_lo, br]
    rows, route, counts = _outproj_router(oa, ob, h0, consts)

    n_tiles = -(-t // TE) + N_CLASSES
    n_tiles = -(-n_tiles // 8) * 8
    slots, tmap = _slot_assign(route, counts, n_tiles)
    slot = slots[:, 0]
    xs = _dispatch(slot, rows, n_tiles * TE)
    z = _expert_mlp(tmap[:, T_ELO], tmap[:, T_EHI], tmap[:, T_USED], xs,
                    w_gate[0].astype(BF16), w_up[0].astype(BF16), w_down[0].astype(BF16),
                    row(ln_ffn_g[0]), row(ln_ffn_b[0]))
    return _undispatch(slot, z).reshape(batch, seq, d)
```

```python
import functools

import numpy as np
import jax
import jax.numpy as jnp
from jax import lax
from jax.experimental import pallas as pl
from jax.experimental.pallas import tpu as pltpu

F32 = jnp.float32
BF16 = jnp.bfloat16

D_MODEL = 1024
A_HEADS = 8
A_KV_HEADS = 2
A_HEAD_DIM = 64
WINDOW = 128
B_HEADS = 8
B_NOPE = 64
B_ROPE = 32
B_V = 64
B_Q_RANK = 768
B_KV_RANK = 256
ROPE_THETA = 10000.0
N_GROUPS = 4
EXPERTS_PER_GROUP = 8
N_EXPERTS = N_GROUPS * EXPERTS_PER_GROUP
EXPERT_FF = 256
LN_EPS = 1e-5
RMS_EPS = 1e-6
DEPTH = 1
ALPHA = (2 * DEPTH) ** 0.25
NEG_BIG = -1e30
LOG2_E = 1.4426950408889634

LANES = 128
A_Q_COLS = A_HEADS * A_HEAD_DIM
A_KV_COLS = A_KV_HEADS * A_HEAD_DIM
PAD_COLS = B_HEADS * LANES
BF16_SUBLANES = 16
VT_ROWS = B_V + BF16_SUBLANES
A_GROUP = A_HEADS // A_KV_HEADS

C_QA = 0
C_KA = C_QA + A_HEADS * LANES
C_CQ = C_KA + A_KV_COLS
C_CKV = C_CQ + B_Q_RANK
C_KR = C_CKV + B_KV_RANK
W1_COLS = C_KR + LANES

TM_IN_PROJ = 512
TM_PROJ = 256
TQ_A = 256
TQ_B = 1024
TK_B = 256
KT_PER_ITER = 32
TE = 384
TM_SLOT = 1024
TM_DISP = 1024
VMEM_LIMIT = 56 * 1024 * 1024


def _cparams(sem):
    return pltpu.CompilerParams(dimension_semantics=sem, vmem_limit_bytes=VMEM_LIMIT)


def _layer_norm(x, g, b):
    mu = jnp.mean(x, -1, keepdims=True)
    xc = x - mu
    var = jnp.mean(xc * xc, -1, keepdims=True)
    return xc * lax.rsqrt(var + LN_EPS) * g + b


def _rope_block(xc, cos, sin_signed, first_half, half):
    rot = jnp.where(first_half, pltpu.roll(xc, LANES - half, 1), pltpu.roll(xc, half, 1))
    return xc * cos + rot * sin_signed


HALF_A = A_HEAD_DIM // 2
HALF_B = B_ROPE // 2


def _rope_values(pos, inv):
    ang = pos.astype(F32) * inv
    cc = jnp.cos(ang)
    ss = jnp.sin(ang)
    lane = lax.broadcasted_iota(jnp.int32, (1, LANES), 1)

    def table_a(x):
        out = x
        for k in range(1, LANES // HALF_A):
            out = jnp.where(lane >= k * HALF_A, pltpu.roll(x, k * HALF_A, 1), out)
        return out

    def table_b(x, fill):
        lo = pltpu.roll(x, B_NOPE - HALF_A, 1)
        hi = pltpu.roll(x, B_NOPE - HALF_A + HALF_B, 1)
        out = jnp.where(lane < B_NOPE + HALF_B, lo, hi)
        return jnp.where((lane >= B_NOPE) & (lane < B_NOPE + B_ROPE), out, fill)

    sign_a = jnp.where(lane % A_HEAD_DIM < HALF_A, -1.0, 1.0)
    sign_b = jnp.where(lane < B_NOPE + HALF_B, -1.0, 1.0)
    return table_a(cc), table_a(ss) * sign_a, table_b(cc, 1.0), table_b(ss, 0.0) * sign_b


def _rope_rates():
    inv_a = 1.0 / (ROPE_THETA ** (np.arange(0, A_HEAD_DIM, 2, dtype=np.float64) / A_HEAD_DIM))
    inv_b = 1.0 / (ROPE_THETA ** (np.arange(0, B_ROPE, 2, dtype=np.float64) / B_ROPE))
    inv = np.zeros((1, LANES))
    inv[0, :HALF_A] = inv_a
    inv[0, HALF_A:HALF_A + HALF_B] = inv_b
    return jnp.asarray(inv, F32)


def _ones_rows(vt):
    vrow = lax.broadcasted_iota(jnp.int32, (vt.shape[0], 1), 0)
    return jnp.where(vrow % VT_ROWS >= B_V, 1.0, vt)


def _proj_body(x_ref, pos_ref, g_ref, b_ref, w1_ref, wva_ref, wq_ref, wk_ref, wv_ref, gq_ref, gkv_ref, inv_ref,
               h0_ref, qa_ref, ka_ref, vat_ref, qb_ref, kb_ref, vt_ref):
    lane = lax.broadcasted_iota(jnp.int32, (1, LANES), 1)
    first_a = (lane % A_HEAD_DIM) < (A_HEAD_DIM // 2)
    first_b = (lane % B_ROPE) < (B_ROPE // 2)
    scale_a = A_HEAD_DIM ** -0.5 * LOG2_E
    scale_b = (B_NOPE + B_ROPE) ** -0.5 * LOG2_E
    h0 = _layer_norm(x_ref[...], g_ref[...], b_ref[...])
    h0_ref[...] = h0
    h = h0.astype(BF16)
    cosa, sina, cosb, sinb = _rope_values(pos_ref[...], inv_ref[...])

    def proj(c0, width):
        return jnp.dot(h, w1_ref[:, c0:c0 + width], preferred_element_type=F32)

    qa = proj(C_QA, A_HEADS * LANES)
    for c in range(A_HEADS):
        blk = _rope_block(qa[:, c * LANES:(c + 1) * LANES], cosa, sina, first_a, HALF_A)
        qa_ref[:, c * LANES:(c + 1) * LANES] = (blk * scale_a).astype(BF16)
    ka_ref[...] = _rope_block(proj(C_KA, A_KV_COLS), cosa, sina, first_a, HALF_A).astype(BF16)
    vat = lax.dot_general(wva_ref[...], h, (((1,), (1,)), ((), ())), preferred_element_type=F32)
    vat_ref[...] = _ones_rows(vat).astype(BF16)

    cq = proj(C_CQ, B_Q_RANK)
    cq = cq * lax.rsqrt(jnp.mean(cq * cq, -1, keepdims=True) + RMS_EPS) * gq_ref[...]
    qb = jnp.dot(cq.astype(BF16), wq_ref[...], preferred_element_type=F32)
    for c in range(B_HEADS):
        blk = _rope_block(qb[:, c * LANES:(c + 1) * LANES], cosb, sinb, first_b, HALF_B)
        qb_ref[:, c * LANES:(c + 1) * LANES] = (blk * scale_b).astype(BF16)

    ckv = proj(C_CKV, B_KV_RANK)
    ckv = (ckv * lax.rsqrt(jnp.mean(ckv * ckv, -1, keepdims=True) + RMS_EPS) * gkv_ref[...]).astype(BF16)
    kr = _rope_block(proj(C_KR, LANES), cosb, sinb, first_b, HALF_B)
    kn = jnp.dot(ckv, wk_ref[...], preferred_element_type=F32)
    for c in range(B_HEADS):
        kb_ref[:, c * LANES:(c + 1) * LANES] = (kn[:, c * LANES:(c + 1) * LANES] + kr).astype(BF16)
    vt = lax.dot_general(wv_ref[...], ckv, (((1,), (1,)), ((), ())), preferred_element_type=F32)
    vt_ref[...] = _ones_rows(vt).astype(BF16)


def _proj_weights(w_in, w_q_b, w_kv_b):
    w = w_in
    d = w.shape[0]
    o = A_Q_COLS
    ka = w[:, o:o + A_KV_COLS]; o += A_KV_COLS
    va = w[:, o:o + A_KV_COLS]; o += A_KV_COLS
    cq = w[:, o:o + B_Q_RANK]; o += B_Q_RANK
    ckv = w[:, o:o + B_KV_RANK]; o += B_KV_RANK
    kr = w[:, o:o + B_ROPE]
    kr_blk = jnp.concatenate([jnp.zeros((d, B_NOPE), w.dtype), kr,
                              jnp.zeros((d, LANES - B_NOPE - B_ROPE), w.dtype)], -1)
    qa = w[:, :A_Q_COLS].reshape(d, A_KV_HEADS, A_GROUP, A_HEAD_DIM)
    qa = jnp.stack([jnp.pad(qa[:, g], ((0, 0), (0, 0), (g * A_HEAD_DIM, LANES - (g + 1) * A_HEAD_DIM)))
                    for g in range(A_KV_HEADS)], 1).reshape(d, A_HEADS * LANES)
    w1 = jnp.concatenate([qa, ka, cq, ckv, kr_blk], -1).astype(BF16)
    wva = jnp.pad(va.reshape(d, A_KV_HEADS, A_HEAD_DIM), ((0, 0), (0, 0), (0, VT_ROWS - A_HEAD_DIM)))
    wva = wva.reshape(d, A_KV_HEADS * VT_ROWS).T.astype(BF16)
    wq = w_q_b.reshape(B_Q_RANK, B_HEADS, B_NOPE + B_ROPE)
    wq = jnp.pad(wq, ((0, 0), (0, 0), (0, LANES - B_NOPE - B_ROPE))).reshape(B_Q_RANK, PAD_COLS).astype(BF16)
    wkv = w_kv_b.reshape(B_KV_RANK, B_HEADS, B_NOPE + B_V)
    wk = jnp.pad(wkv[:, :, :B_NOPE], ((0, 0), (0, 0), (0, LANES - B_NOPE))).reshape(B_KV_RANK, PAD_COLS).astype(BF16)
    wv = jnp.pad(wkv[:, :, B_NOPE:], ((0, 0), (0, 0), (0, VT_ROWS - B_V)))
    wv = wv.reshape(B_KV_RANK, B_HEADS * VT_ROWS).T.astype(BF16)
    return w1, wva, wq, wk, wv


def _projections(x2, pos, ln_g, ln_b, w1, wva, wq, wk, wv, gq, gkv):
    t = x2.shape[0]
    tm = TM_IN_PROJ
    full = lambda a: pl.BlockSpec(a.shape, lambda i: (0,) * a.ndim)
    tile = lambda n: pl.BlockSpec((tm, n), lambda i: (i, 0))
    consts = [ln_g, ln_b, w1, wva, wq, wk, wv, gq, gkv, _rope_rates()]
    tok = lambda n, dt=BF16: (tile(n), jax.ShapeDtypeStruct((t, n), dt))
    feat = lambda r: (pl.BlockSpec((r, tm), lambda i: (0, i)), jax.ShapeDtypeStruct((r, t), BF16))
    outs = [tok(D_MODEL, F32), tok(A_HEADS * LANES), tok(A_KV_COLS), feat(A_KV_HEADS * VT_ROWS),
            tok(PAD_COLS), tok(PAD_COLS), feat(B_HEADS * VT_ROWS)]
    return pl.pallas_call(
        _proj_body,
        grid=(t // tm,),
        in_specs=[tile(D_MODEL), tile(1)] + [full(a) for a in consts],
        out_specs=[o[0] for o in outs],
        out_shape=[o[1] for o in outs],
        compiler_params=_cparams(("parallel",)),
        name="ln_in_proj",
    )(x2, pos, *consts)


def _attn_a_body(seq, sink_ref, q_ref, k0, k1, k2, k3, v0, v1, v2, v3, o_ref, s_sc):
    i = pl.program_id(1)
    kk = jnp.concatenate([k0[...], k1[...], k2[...], k3[...]], 0)
    vt = jnp.concatenate([v0[...], v1[...], v2[...], v3[...]], 1)
    nk = kk.shape[0]
    kpos = i * TQ_A - WINDOW + lax.broadcasted_iota(jnp.int32, (nk, 1), 0)
    qpos = i * TQ_A + lax.broadcasted_iota(jnp.int32, (1, TQ_A), 1)
    valid = (kpos >= 0) & (kpos < seq) & (jnp.abs(kpos - qpos) <= WINDOW)
    bias = jnp.where(valid, 0.0, NEG_BIG)

    def scores(h):
        qt = q_ref[:, h * LANES:(h + 1) * LANES].astype(F32).T.astype(BF16)
        s = jnp.dot(kk, qt, preferred_element_type=F32) + bias
        return s, jnp.max(s, 0, keepdims=True)

    s0, m_tile = scores(0)
    s_sc[0] = s0
    for h in range(A_HEADS):
        g = h // A_GROUP
        if h + 1 < A_HEADS:
            s_next, mt_next = scores(h + 1)
            s_sc[(h + 1) % 2] = s_next
        sink = sink_ref[0, h] * LOG2_E
        m = jnp.maximum(m_tile, sink)
        p = jnp.exp2(s_sc[h % 2] - m).astype(BF16)
        acc = jnp.dot(vt[g * VT_ROWS:(g + 1) * VT_ROWS, :], p, preferred_element_type=F32)
        den = acc[A_HEAD_DIM:A_HEAD_DIM + 1, :] + jnp.exp2(sink - m)
        o_ref[h * A_HEAD_DIM:(h + 1) * A_HEAD_DIM, :] = acc[:A_HEAD_DIM, :] / den
        if h + 1 < A_HEADS:
            m_tile = mt_next


def _attention_a(qa, ka, vat, sink, batch, seq):
    t = qa.shape[0]
    nq = seq // TQ_A
    nb = seq // WINDOW
    per = TQ_A // WINDOW
    assert per + 2 == 4

    def halo(j):
        return lambda b, i: b * nb + jnp.clip(per * i - 1 + j, 0, nb - 1)

    k_specs = [pl.BlockSpec((WINDOW, A_KV_COLS), lambda b, i, f=halo(j): (f(b, i), 0)) for j in range(per + 2)]
    v_specs = [pl.BlockSpec((A_KV_HEADS * VT_ROWS, WINDOW), lambda b, i, f=halo(j): (0, f(b, i)))
               for j in range(per + 2)]
    return pl.pallas_call(
        functools.partial(_attn_a_body, seq),
        grid=(batch, nq),
        in_specs=[pl.BlockSpec(memory_space=pltpu.SMEM),
                  pl.BlockSpec((TQ_A, A_HEADS * LANES), lambda b, i: (b * nq + i, 0))] + k_specs + v_specs,
        out_specs=pl.BlockSpec((A_Q_COLS, TQ_A), lambda b, i: (0, b * nq + i)),
        out_shape=jax.ShapeDtypeStruct((A_Q_COLS, t), F32),
        scratch_shapes=[pltpu.VMEM((2, TQ_A + 2 * WINDOW, TQ_A), F32)],
        compiler_params=_cparams(("parallel", "parallel")),
        name="window_attn",
    )(sink, qa, *([ka] * (per + 2)), *([vat] * (per + 2)))


def _attn_b_body(nkt, q_ref, k_ref, vt_ref, o_ref, s_sc, acc_sc):
    qt = q_ref[...].astype(F32).T.astype(BF16)
    acc_sc[...] = jnp.zeros(acc_sc.shape, F32)

    def scores(j):
        start = pl.multiple_of(j * TK_B, TK_B)
        s = jnp.dot(k_ref[pl.ds(start, TK_B), :], qt, preferred_element_type=F32)
        return s, jnp.max(s, 0, keepdims=True)

    def update(j, s, m, m_tile):
        start = pl.multiple_of(j * TK_B, TK_B)
        m_new = jnp.maximum(m, m_tile)
        alpha = jnp.exp2(m - m_new)
        p = jnp.exp2(s - m_new).astype(BF16)
        pv = jnp.dot(vt_ref[:, pl.ds(start, TK_B)], p, preferred_element_type=F32)
        acc_sc[...] = alpha * acc_sc[...] + pv
        return m_new

    s0, mt0 = scores(0)
    s_sc[0] = s0

    per_iter = min(KT_PER_ITER, nkt)
    assert nkt % per_iter == 0 and per_iter % 2 == 0

    def group(jj, carry):
        m, m_tile = carry
        for u in range(per_iter):
            j = per_iter * jj + u
            s_next, mt_next = scores(jnp.minimum(j + 1, nkt - 1))
            s_sc[(u + 1) % 2] = s_next
            m = update(j, s_sc[u % 2], m, m_tile)
            m_tile = mt_next
        return m, m_tile

    m0 = jnp.full((1, TQ_B), -jnp.inf, F32)
    lax.fori_loop(0, nkt // per_iter, group, (m0, mt0))
    o_ref[...] = acc_sc[:B_V, :] / acc_sc[B_V:B_V + 1, :]


def _attention_b(qb, kb, vt, batch, seq):
    t = qb.shape[0]
    nq = seq // TQ_B
    nkt = seq // TK_B
    return pl.pallas_call(
        functools.partial(_attn_b_body, nkt),
        grid=(batch * B_HEADS, nq),
        in_specs=[pl.BlockSpec((TQ_B, LANES), lambda bh, i: ((bh // B_HEADS) * nq + i, bh % B_HEADS)),
                  pl.BlockSpec((seq, LANES), lambda bh, i: (bh // B_HEADS, bh % B_HEADS)),
                  pl.BlockSpec((VT_ROWS, seq), lambda bh, i: (bh % B_HEADS, bh // B_HEADS))],
        out_specs=pl.BlockSpec((B_V, TQ_B), lambda bh, i: (bh % B_HEADS, (bh // B_HEADS) * nq + i)),
        out_shape=jax.ShapeDtypeStruct((B_HEADS * B_V, t), F32),
        scratch_shapes=[pltpu.VMEM((2, TK_B, TQ_B), F32), pltpu.VMEM((VT_ROWS, TQ_B), F32)],
        compiler_params=_cparams(("parallel", "arbitrary")),
        name="mla_attn",
    )(qb, kb, vt)


R_CLASS, R_RANK = range(2)
M_WLO, M_WHI = range(2)
GROUP_LANE0 = N_EXPERTS
PAIRS_PER_GROUP = EXPERTS_PER_GROUP * (EXPERTS_PER_GROUP - 1) // 2
N_CLASSES = N_GROUPS * PAIRS_PER_GROUP
ROW_WORDS = D_MODEL + LANES


def _lane_pick(slab, lane, idx):
    return jnp.sum(jnp.where(lane == idx, slab, 0.0), -1, keepdims=True)


def _outproj_body(oa_ref, ob_ref, h0_ref, ga_ref, gb_ref, woa_ref, wob_ref, lg1_ref, lb1_ref,
                  wrh_ref, wrl_ref, br_ref, rows_ref, route_ref, counts_ref, carry_sc):
    @pl.when(pl.program_id(0) == 0)
    def _():
        carry_sc[...] = jnp.zeros(carry_sc.shape, F32)

    oa = oa_ref[...].T
    ob = ob_ref[...].T
    na = oa * lax.rsqrt(jnp.mean(oa * oa, -1, keepdims=True) + RMS_EPS) * ga_ref[...]
    nb = ob * lax.rsqrt(jnp.mean(ob * ob, -1, keepdims=True) + RMS_EPS) * gb_ref[...]
    mixed = (jnp.dot(na.astype(BF16), woa_ref[...], preferred_element_type=F32)
             + jnp.dot(nb.astype(BF16), wob_ref[...], preferred_element_type=F32))
    h1 = _layer_norm(ALPHA * h0_ref[...] + mixed, lg1_ref[...], lb1_ref[...])
    rows_ref[:, :D_MODEL] = h1

    hi = h1.astype(BF16)
    lo = (h1 - hi.astype(F32)).astype(BF16)
    logits = (jnp.dot(hi, wrh_ref[...], preferred_element_type=F32)
              + jnp.dot(lo, wrh_ref[...], preferred_element_type=F32)
              + jnp.dot(hi, wrl_ref[...], preferred_element_type=F32)) + br_ref[...]

    tm = logits.shape[0]
    lane = lax.broadcasted_iota(jnp.int32, (1, LANES), 1).astype(F32)
    far = float(LANES)
    is_g = (lane >= GROUP_LANE0) & (lane < GROUP_LANE0 + N_GROUPS)
    gl = jnp.where(is_g, logits, -jnp.inf)
    gmax = jnp.max(gl, -1, keepdims=True)
    gidx = jnp.min(jnp.where(gl == gmax, lane, far), -1, keepdims=True) - GROUP_LANE0
    pg = 1.0 / jnp.sum(jnp.where(is_g, jnp.exp(jnp.where(is_g, logits, gmax) - gmax), 0.0), -1, keepdims=True)
    in_grp = (lane >= gidx * EXPERTS_PER_GROUP) & (lane < (gidx + 1.0) * EXPERTS_PER_GROUP)
    el = jnp.where(in_grp, logits, -jnp.inf)
    v1 = jnp.max(el, -1, keepdims=True)
    i1 = jnp.min(jnp.where(el == v1, lane, far), -1, keepdims=True)
    el2 = jnp.where(lane == i1, -jnp.inf, el)
    v2 = jnp.max(el2, -1, keepdims=True)
    i2 = jnp.min(jnp.where(el2 == v2, lane, far), -1, keepdims=True)
    ex = jnp.exp(v2 - v1)
    w1 = pg / (1.0 + ex)
    w2 = pg * ex / (1.0 + ex)

    first_lo = i1 < i2
    a = jnp.minimum(i1, i2) - gidx * EXPERTS_PER_GROUP
    b = jnp.maximum(i1, i2) - gidx * EXPERTS_PER_GROUP
    pair = a * (2.0 * EXPERTS_PER_GROUP - 1.0 - a) * 0.5 + (b - a - 1.0)
    cls = gidx * PAIRS_PER_GROUP + pair
    w_lo = jnp.where(first_lo, w1, w2)
    w_hi = jnp.where(first_lo, w2, w1)
    rows_ref[:, D_MODEL:] = jnp.where(lane == M_WLO, w_lo, jnp.where(lane == M_WHI, w_hi, 0.0))

    onehot = jnp.where(lane == cls, 1.0, 0.0)
    r_i = lax.broadcasted_iota(jnp.int32, (tm, tm), 0)
    c_i = lax.broadcasted_iota(jnp.int32, (tm, tm), 1)
    lower = jnp.where(c_i < r_i, 1.0, 0.0).astype(BF16)
    prefix = jnp.dot(lower, onehot.astype(BF16), preferred_element_type=F32) + carry_sc[0:1, :]
    rank = _lane_pick(prefix, lane, cls)
    new_carry = carry_sc[0:1, :] + jnp.sum(onehot, 0, keepdims=True)
    carry_sc[...] = jnp.broadcast_to(new_carry, carry_sc.shape)
    counts_ref[...] = jnp.broadcast_to(new_carry, counts_ref.shape)
    route_ref[...] = jnp.where(lane == R_CLASS, cls, jnp.where(lane == R_RANK, rank, 0.0))


def _outproj_router(oa, ob, x2, consts):
    t = x2.shape[0]
    tm = TM_PROJ
    full = lambda a: pl.BlockSpec(a.shape, lambda i: (0,) * a.ndim)
    tile = lambda n: pl.BlockSpec((tm, n), lambda i: (i, 0))
    return pl.pallas_call(
        _outproj_body,
        grid=(t // tm,),
        in_specs=[pl.BlockSpec((A_Q_COLS, tm), lambda i: (0, i)),
                  pl.BlockSpec((B_HEADS * B_V, tm), lambda i: (0, i)), tile(D_MODEL)]
        + [full(a) for a in consts],
        out_specs=[tile(ROW_WORDS), tile(LANES), pl.BlockSpec((8, LANES), lambda i: (0, 0))],
        out_shape=[jax.ShapeDtypeStruct((t, ROW_WORDS), F32),
                   jax.ShapeDtypeStruct((t, LANES), F32), jax.ShapeDtypeStruct((8, LANES), F32)],
        scratch_shapes=[pltpu.VMEM((8, LANES), F32)],
        compiler_params=_cparams(("arbitrary",)),
        name="out_proj_router",
    )(oa, ob, x2, *consts)


T_CLASS, T_USED, T_ELO, T_EHI = range(4)


def _slot_body(n_tiles, route_ref, counts_ref, elo_ref, ehi_ref, slots_ref, tmap_ref):
    lane_i = lax.broadcasted_iota(jnp.int32, (1, LANES), 1)
    lane = lane_i.astype(F32)
    is_c = lane_i < N_CLASSES
    cnt = jnp.where(is_c, counts_ref[0:1, :], 0.0)
    padded = jnp.floor((cnt + (TE - 1)) * (1.0 / TE)) * TE
    incl = padded
    k = 1
    while k < LANES:
        incl = incl + jnp.where(lane_i >= k, pltpu.roll(incl, k, 1), 0.0)
        k *= 2
    off = incl - padded

    route = route_ref[...]
    cls = _lane_pick(route, lane, R_CLASS)
    slot = _lane_pick(jnp.broadcast_to(off, route.shape), lane, cls) + _lane_pick(route, lane, R_RANK)
    slots_ref[...] = jnp.where(lane == 0, slot, 0.0).astype(jnp.int32)

    row0 = lax.broadcasted_iota(jnp.int32, (n_tiles, 1), 0).astype(F32) * TE
    tc = jnp.sum(jnp.where(is_c & (incl <= row0), 1.0, 0.0), -1, keepdims=True)
    tc = jnp.minimum(tc, N_CLASSES - 1.0)
    at_tc = lane == tc
    used = jnp.clip(jnp.sum(jnp.where(at_tc, cnt - (row0 - off), 0.0), -1, keepdims=True), 0.0, float(TE))
    elo = jnp.sum(jnp.where(at_tc, elo_ref[...], 0.0), -1, keepdims=True)
    ehi = jnp.sum(jnp.where(at_tc, ehi_ref[...], 0.0), -1, keepdims=True)
    tmap = jnp.zeros((n_tiles, LANES), F32)
    for idx, val in ((T_CLASS, tc), (T_USED, used), (T_ELO, elo), (T_EHI, ehi)):
        tmap = jnp.where(lane == idx, val, tmap)
    tmap_ref[...] = tmap.astype(jnp.int32)


def _class_experts():
    elo = np.zeros((1, LANES))
    ehi = np.zeros((1, LANES))
    c = 0
    for g in range(N_GROUPS):
        for a in range(EXPERTS_PER_GROUP):
            for b in range(a + 1, EXPERTS_PER_GROUP):
                elo[0, c] = g * EXPERTS_PER_GROUP + a
                ehi[0, c] = g * EXPERTS_PER_GROUP + b
                c += 1
    return jnp.asarray(elo, F32), jnp.asarray(ehi, F32)


def _slot_assign(route, counts, n_tiles):
    t = route.shape[0]
    tm = min(TM_SLOT, t)
    row = pl.BlockSpec((1, LANES), lambda i: (0, 0))
    return pl.pallas_call(
        functools.partial(_slot_body, n_tiles),
        grid=(t // tm,),
        in_specs=[pl.BlockSpec((tm, LANES), lambda i: (i, 0)), pl.BlockSpec((8, LANES), lambda i: (0, 0)), row, row],
        out_specs=[pl.BlockSpec((tm, LANES), lambda i: (i, 0)), pl.BlockSpec((n_tiles, LANES), lambda i: (0, 0))],
        out_shape=[jax.ShapeDtypeStruct((t, LANES), jnp.int32), jax.ShapeDtypeStruct((n_tiles, LANES), jnp.int32)],
        compiler_params=_cparams(("arbitrary",)),
        name="slot_assign",
    )(route, counts, *_class_experts())


def _dispatch_body(tm, slots_ref, src_ref, zeros_ref, dst_ref, sem):
    del zeros_ref

    def row_copy(i, u):
        return pltpu.make_async_copy(src_ref.at[i, pl.ds(u, 1), :],
                                     dst_ref.at[pl.ds(slots_ref[i * ROW_GROUP + u], 1), :], sem)

    _row_dma_loops(tm, row_copy)


def _dispatch(slots, rows, n_rows):
    t = rows.shape[0]
    tm = min(TM_DISP, t)
    return pl.pallas_call(
        functools.partial(_dispatch_body, tm),
        grid=(t // tm,),
        in_specs=[pl.BlockSpec((tm,), lambda i: (i,), memory_space=pltpu.SMEM),
                  pl.BlockSpec((tm // ROW_GROUP, ROW_GROUP, ROW_WORDS), lambda i: (i, 0, 0)),
                  pl.BlockSpec(memory_space=pl.ANY)],
        out_specs=pl.BlockSpec(memory_space=pl.ANY),
        out_shape=jax.ShapeDtypeStruct((n_rows, ROW_WORDS), F32),
        scratch_shapes=[pltpu.SemaphoreType.DMA(())],
        input_output_aliases={2: 0},
        compiler_params=_cparams(("arbitrary",)),
        name="moe_dispatch",
    )(slots, rows.reshape(t // ROW_GROUP, ROW_GROUP, ROW_WORDS), jnp.zeros((n_rows, ROW_WORDS), F32))


def _expert_body(elo_ref, ehi_ref, used_ref, src_ref, rows_ref, wg0_ref, wu0_ref, wd0_ref, wg1_ref, wu1_ref, wd1_ref,
                 g_ref, b_ref, z_ref):
    del elo_ref, ehi_ref, src_ref
    used = used_ref[pl.program_id(0)]

    @pl.when(used > 0)
    def _():
        h1 = rows_ref[:, :D_MODEL]
        meta = rows_ref[:, D_MODEL:]
        lane = lax.broadcasted_iota(jnp.int32, (1, LANES), 1).astype(F32)
        x = h1.astype(BF16)
        ffn = jnp.zeros((TE, D_MODEL), F32)
        for (wg, wu, wd), lane_w in (((wg0_ref, wu0_ref, wd0_ref), M_WLO), ((wg1_ref, wu1_ref, wd1_ref), M_WHI)):
            g = jnp.dot(x, wg[0], preferred_element_type=F32)
            u = jnp.dot(x, wu[0], preferred_element_type=F32)
            hid = (g * (1.0 / (1.0 + jnp.exp(-g))) * u).astype(BF16)
            y = jnp.dot(hid, wd[0], preferred_element_type=F32)
            ffn = ffn + _lane_pick(meta, lane, lane_w) * y
        z_ref[...] = _layer_norm(ALPHA * h1 + ffn, g_ref[...], b_ref[...])

    @pl.when(used <= 0)
    def _():
        z_ref[...] = jnp.zeros(z_ref.shape, F32)


def _expert_mlp(tile_elo, tile_ehi, tile_used, xs, w_gate, w_up, w_down, ln_g, ln_b):
    n_rows = xs.shape[0]
    n_tiles = n_rows // TE
    last_used = jnp.maximum(jnp.sum((tile_used > 0).astype(jnp.int32)) - 1, 0)
    tile_src = jnp.minimum(jnp.arange(n_tiles, dtype=jnp.int32), last_used)
    up_spec = lambda which: pl.BlockSpec((1, D_MODEL, EXPERT_FF),
                                         lambda r, lo, hi, us, src: ((lo, hi)[which][r], 0, 0))
    down_spec = lambda which: pl.BlockSpec((1, EXPERT_FF, D_MODEL),
                                           lambda r, lo, hi, us, src: ((lo, hi)[which][r], 0, 0))
    row = pl.BlockSpec((1, D_MODEL), lambda r, lo, hi, us, src: (0, 0))
    grid_spec = pltpu.PrefetchScalarGridSpec(
        num_scalar_prefetch=4,
        grid=(n_tiles,),
        in_specs=[pl.BlockSpec((TE, ROW_WORDS), lambda r, lo, hi, us, src: (src[r], 0)),
                  up_spec(0), up_spec(0), down_spec(0), up_spec(1), up_spec(1), down_spec(1), row, row],
        out_specs=pl.BlockSpec((TE, D_MODEL), lambda r, lo, hi, us, src: (r, 0)),
    )
    return pl.pallas_call(
        _expert_body,
        grid_spec=grid_spec,
        out_shape=jax.ShapeDtypeStruct((n_rows, D_MODEL), F32),
        compiler_params=pltpu.CompilerParams(
            dimension_semantics=("arbitrary",), vmem_limit_bytes=VMEM_LIMIT,
            allow_input_fusion=[False] * 5 + [True] * 6 + [False] * 2),
        name="expert_mlp",
    )(tile_elo, tile_ehi, tile_used, tile_src, xs, w_gate, w_up, w_down, w_gate, w_up, w_down, ln_g, ln_b)


ROW_GROUP = 8


def _row_dma_loops(tm, row_copy):
    def start(i, carry):
        for u in range(ROW_GROUP):
            row_copy(i, u).start()
        return carry

    def wait(i, carry):
        for u in range(ROW_GROUP):
            row_copy(i, u).wait()
        return carry

    lax.fori_loop(0, tm // ROW_GROUP, start, 0)
    lax.fori_loop(0, tm // ROW_GROUP, wait, 0)


def _undispatch_body(tm, slots_ref, z_ref, o_ref, sem):
    def row_copy(i, u):
        return pltpu.make_async_copy(z_ref.at[pl.ds(slots_ref[i * ROW_GROUP + u], 1), :],
                                     o_ref.at[i, pl.ds(u, 1), :], sem)

    _row_dma_loops(tm, row_copy)


def _undispatch(slots, z):
    t = slots.shape[0]
    tm = min(TM_DISP, t)
    return pl.pallas_call(
        functools.partial(_undispatch_body, tm),
        grid=(t // tm,),
        in_specs=[pl.BlockSpec((tm,), lambda i: (i,), memory_space=pltpu.SMEM),
                  pl.BlockSpec(memory_space=pl.ANY)],
        out_specs=pl.BlockSpec((tm // ROW_GROUP, ROW_GROUP, D_MODEL), lambda i: (i, 0, 0)),
        out_shape=jax.ShapeDtypeStruct((t // ROW_GROUP, ROW_GROUP, D_MODEL), F32),
        scratch_shapes=[pltpu.SemaphoreType.DMA(())],
        compiler_params=_cparams(("arbitrary",)),
        name="moe_undispatch",
    )(slots, z)


def kernel(x, positions, ln_emb_g, ln_emb_b, w_in, a_sink, q_a_norm_g, w_q_b, kv_a_norm_g, w_kv_b, out_norm_a_g, out_norm_b_g, w_out, ln_attn_g, ln_attn_b, w_group, b_group, w_expert, b_expert, w_gate, w_up, w_down, ln_ffn_g, ln_ffn_b):
    batch, seq, d = x.shape
    t = batch * seq
    x2 = x.reshape(t, d)
    row = lambda a: a.reshape(1, -1).astype(F32)

    w1, wva, wq, wk, wv = _proj_weights(w_in[0], w_q_b[0], w_kv_b[0])
    h0, qa, ka, vat, qb, kb, vt = _projections(x2, positions.reshape(t, 1), row(ln_emb_g), row(ln_emb_b),
                                               w1, wva, wq, wk, wv, row(q_a_norm_g[0]), row(kv_a_norm_g[0]))
    oa = _attention_a(qa, ka, vat, a_sink[0].reshape(1, A_HEADS).astype(F32), batch, seq)
    ob = _attention_b(qb, kb, vt, batch, seq)

    wo = w_out[0]
    woa = wo[:A_Q_COLS].astype(BF16)
    wob = wo[A_Q_COLS:].astype(BF16)
    gb = row(out_norm_b_g[0])
    wr = jnp.concatenate([w_expert[0], w_group[0],
                          jnp.zeros((d, LANES - N_EXPERTS - N_GROUPS), F32)], -1)
    wr_hi = wr.astype(BF16)
    wr_lo = (wr - wr_hi.astype(F32)).astype(BF16)
    br = jnp.concatenate([b_expert[0], b_group[0], jnp.zeros((LANES - N_EXPERTS - N_GROUPS,), F32)]).reshape(1, LANES)
    consts = [row(out_norm_a_g[0]), gb, woa, wob, row(ln_attn_g[0]), row(ln_attn_b[0]),
              wr_hi, wr_lo, br]
    rows, route, counts = _outproj_router(oa, ob, h0, consts)

    n_tiles = -(-t // TE) + N_CLASSES
    n_tiles = -(-n_tiles // 8) * 8
    slots, tmap = _slot_assign(route, counts, n_tiles)
    slot = slots[:, 0]
    xs = _dispatch(slot, rows, n_tiles * TE)
    z = _expert_mlp(tmap[:, T_ELO], tmap[:, T_EHI], tmap[:, T_USED], xs,
                    w_gate[0].astype(BF16), w_up[0].astype(BF16), w_down[0].astype(BF16),
                    row(ln_ffn_g[0]), row(ln_ffn_b[0]))
    return _undispatch(slot, z).reshape(batch, seq, d)
```
